```python
import math
import jax, jax.numpy as jnp
from jax import lax
import numpy as np

D_MODEL = 1024
BATCH = 8
SEQ = 2048
DEPTH = 4

N_A = DEPTH // 2
N_B = DEPTH - N_A
POOL_WINDOWS = (2, 4, 8, 16)
N_GROUPS = len(POOL_WINDOWS)
GROUP_DIM = D_MODEL // N_GROUPS
N_HEADS = 8
HEAD_DIM = D_MODEL // (2 * N_HEADS)
V_DIM = 2 * HEAD_DIM
QK_WIDTH = N_HEADS * 2 * HEAD_DIM
V_WIDTH = N_HEADS * V_DIM
D_FF = 4 * D_MODEL
N_BUCKETS = 32
MAX_DISTANCE = 128
Q_BLOCK = 128
EPS = 1e-6

kernel_name = "yoco_pool_diffattn_hybrid"


def _rmsnorm(x, g):
    xf = x.astype(jnp.float32)
    y = xf * lax.rsqrt(jnp.mean(xf * xf, axis=-1, keepdims=True) + EPS)
    return (y * g.astype(jnp.float32)).astype(x.dtype)


def _t5_bucket(rel):
    n = jnp.maximum(rel, 0)
    max_exact = N_BUCKETS // 2
    is_small = n < max_exact
    nf = jnp.maximum(n, 1).astype(jnp.float32)
    large = max_exact + (jnp.log(nf / max_exact) / math.log(MAX_DISTANCE / max_exact)
                         * (N_BUCKETS - max_exact)).astype(jnp.int32)
    large = jnp.minimum(large, N_BUCKETS - 1)
    return jnp.where(is_small, n, large)


def _pool_mixer(h, pool_w, pool_scale):
    b, s, _ = h.shape
    hg = h.reshape(b, s, N_GROUPS, GROUP_DIM).astype(jnp.float32)
    c = jnp.cumsum(hg, axis=1)
    c = jnp.concatenate([jnp.zeros((b, 1, N_GROUPS, GROUP_DIM), jnp.float32), c], axis=1)
    t = jnp.arange(s)
    outs = []
    for g, w in enumerate(POOL_WINDOWS):
        lo = jnp.maximum(t + 1 - w, 0)
        win_sum = c[:, 1:, g] - c[:, lo, g]
        cnt = (t + 1 - lo).astype(jnp.float32)
        outs.append(win_sum / cnt[None, :, None])
    pooled = jnp.stack(outs, axis=2)
    m = (pooled - hg).astype(h.dtype)
    y = jnp.einsum('bsgc,gcd->bsgd', m, pool_w).reshape(b, s, D_MODEL)
    return y * pool_scale


def _mlp(h, w_in, w_out):
    a = jnp.square(jax.nn.relu(h @ w_in))
    return a @ w_out


def _diff_attn(h, k, v, w_q, q_norm, lam_q1, lam_k1, lam_q2, lam_k2, subln, w_o, rel_bias, lambda_init):
    b, s, _ = h.shape
    q = (h @ w_q).reshape(b, s, N_HEADS, 2, HEAD_DIM)
    q = _rmsnorm(q, q_norm) * (HEAD_DIM ** -0.5)
    lam = (jnp.exp(jnp.sum(lam_q1.astype(jnp.float32) * lam_k1.astype(jnp.float32)))
           - jnp.exp(jnp.sum(lam_q2.astype(jnp.float32) * lam_k2.astype(jnp.float32)))
           + lambda_init)
    nb = s // Q_BLOCK
    qb = q.reshape(b, nb, Q_BLOCK, N_HEADS, 2, HEAD_DIM).transpose(1, 0, 2, 3, 4, 5)
    kpos = jnp.arange(s)

    def block(args):
        i, qi = args
        qpos = i * Q_BLOCK + jnp.arange(Q_BLOCK)
        rel = qpos[:, None] - kpos[None, :]
        bias = rel_bias[_t5_bucket(rel)].astype(jnp.float32).transpose(2, 0, 1)
        sc = jnp.einsum('bqhcd,bkhcd->bhcqk', qi, k).astype(jnp.float32)
        sc = sc + bias[None, :, None]
        sc = jnp.where((rel >= 0)[None, None, None], sc, -jnp.inf)
        p = jax.nn.softmax(sc, axis=-1)
        a = p[:, :, 0] - lam * p[:, :, 1]
        return jnp.einsum('bhqk,bkhe->bqhe', a.astype(v.dtype), v)

    o = lax.map(block, (jnp.arange(nb), qb))
    o = o.transpose(1, 0, 2, 3, 4).reshape(b, s, N_HEADS, V_DIM)
    o = _rmsnorm(o, subln) * (1.0 - lambda_init)
    return o.reshape(b, s, V_WIDTH) @ w_o


def setup_inputs(seed: int = 0) -> dict:
    key = jax.random.key(seed)
    ks = jax.random.split(key, 24)
    f = jnp.float32
    nrm = lambda k, shape, sc: jax.random.normal(k, shape, f) * sc
    return {
        "x": nrm(ks[0], (BATCH, SEQ, D_MODEL), 1.0),
        "norm_mix": 1.0 + nrm(ks[1], (DEPTH, D_MODEL), 0.05),
        "norm_mlp": 1.0 + nrm(ks[2], (DEPTH, D_MODEL), 0.05),
        "pool_w": nrm(ks[3], (N_A, N_GROUPS, GROUP_DIM, GROUP_DIM), GROUP_DIM ** -0.5),
        "pool_scale": 1.0 + nrm(ks[4], (N_A, D_MODEL), 0.05),
        "kv_norm": 1.0 + nrm(ks[5], (D_MODEL,), 0.05),
        "w_kv": nrm(ks[6], (D_MODEL, QK_WIDTH + V_WIDTH), D_MODEL ** -0.5),
        "k_norm": 1.0 + nrm(ks[7], (HEAD_DIM,), 0.05),
        "rel_bias": nrm(ks[8], (N_BUCKETS, N_HEADS), 0.5),
        "w_q": nrm(ks[9], (N_B, D_MODEL, QK_WIDTH), D_MODEL ** -0.5),
        "q_norm": 1.0 + nrm(ks[10], (N_B, HEAD_DIM), 0.05),
        "lam_q1": nrm(ks[11], (N_B, HEAD_DIM), 0.1),
        "lam_k1": nrm(ks[12], (N_B, HEAD_DIM), 0.1),
        "lam_q2": nrm(ks[13], (N_B, HEAD_DIM), 0.1),
        "lam_k2": nrm(ks[14], (N_B, HEAD_DIM), 0.1),
        "subln": 1.0 + nrm(ks[15], (N_B, V_DIM), 0.05),
        "w_o": nrm(ks[16], (N_B, V_WIDTH, D_MODEL), V_WIDTH ** -0.5),
        "w_mlp_in": nrm(ks[17], (DEPTH, D_MODEL, D_FF), D_MODEL ** -0.5),
        "w_mlp_out": nrm(ks[18], (DEPTH, D_FF, D_MODEL), D_FF ** -0.5),
    }


def reference(x, norm_mix, norm_mlp, pool_w, pool_scale, kv_norm, w_kv, k_norm, rel_bias,
              w_q, q_norm, lam_q1, lam_k1, lam_q2, lam_k2, subln, w_o, w_mlp_in, w_mlp_out):
    b, s, _ = x.shape
    k_sh = None
    v_sh = None
    for l in range(DEPTH):
        h = _rmsnorm(x, norm_mix[l])
        if l < N_A:
            x = x + _pool_mixer(h, pool_w[l], pool_scale[l])
        else:
            j = l - N_A
            lambda_init = 0.8 - 0.6 * math.exp(-0.3 * l)
            x = x + _diff_attn(h, k_sh, v_sh, w_q[j], q_norm[j], lam_q1[j], lam_k1[j],
                               lam_q2[j], lam_k2[j], subln[j], w_o[j], rel_bias, lambda_init)
        x = x + _mlp(_rmsnorm(x, norm_mlp[l]), w_mlp_in[l], w_mlp_out[l])
        if l == N_A - 1:
            kv = _rmsnorm(x, kv_norm) @ w_kv
            k_sh = _rmsnorm(kv[..., :QK_WIDTH].reshape(b, s, N_HEADS, 2, HEAD_DIM), k_norm)
            v_sh = kv[..., QK_WIDTH:].reshape(b, s, N_HEADS, V_DIM)
    return x
```

```python
import functools
import math

import numpy as np
import jax
import jax.numpy as jnp
from jax import lax
from jax.experimental import pallas as pl
from jax.experimental.pallas import tpu as pltpu

D_MODEL = 1024
DEPTH = 4
N_A = DEPTH // 2
POOL_WINDOWS = (2, 4, 8, 16)
N_GROUPS = len(POOL_WINDOWS)
GROUP_DIM = D_MODEL // N_GROUPS
N_HEADS = 8
HEAD_DIM = D_MODEL // (2 * N_HEADS)
V_DIM = 2 * HEAD_DIM
QK_WIDTH = N_HEADS * 2 * HEAD_DIM
D_FF = 4 * D_MODEL
N_BUCKETS = 32
MAX_DISTANCE = 128
EPS = 1e-6

LOG2E = math.log2(math.e)
MASK_VALUE = -1e30
MAX_WINDOW = max(POOL_WINDOWS)

VMEM_LIMIT_BYTES = 56 * 1024 * 1024

POOL_SEQ_TILE = 512
TOKEN_TILE = 512
FF_CHUNK = 1024
ATTN_TILE = 256

F32 = jnp.float32
BF16 = jnp.bfloat16


def _bucket_thresholds():
    max_exact = N_BUCKETS // 2
    n = np.arange(1, 4 * MAX_DISTANCE, dtype=np.float64)
    v = np.log(n / max_exact) / math.log(MAX_DISTANCE / max_exact) * (N_BUCKETS - max_exact)
    frac = v - np.floor(v)
    margin = np.minimum(frac, 1.0 - frac)[max_exact:MAX_DISTANCE - 1]
    assert margin.min() > 1e-3
    large = np.minimum(max_exact + v.astype(np.int64), N_BUCKETS - 1)
    bucket = np.where(n < max_exact, n.astype(np.int64), large)
    thr = [0] + [int(np.argmax(bucket >= b)) + 1 for b in range(1, N_BUCKETS)]
    return tuple(thr)


BUCKET_START = _bucket_thresholds()
FAR_DISTANCE = BUCKET_START[-1]


def _rmsnorm(xf, g):
    ms = jnp.mean(xf * xf, axis=-1, keepdims=True)
    return xf * lax.rsqrt(ms + EPS) * g


def _dot(a, b):
    return jnp.dot(a, b, preferred_element_type=F32)


def _pool_kernel(x_ref, g_ref, pw_ref, ps_ref, o_ref, carry_ref, *, seq_tile):
    si = pl.program_id(1)

    @pl.when(si == 0)
    def _():
        carry_ref[...] = jnp.zeros_like(carry_ref)

    x = x_ref[0]
    h = _rmsnorm(x, g_ref[...])
    ext = jnp.concatenate([carry_ref[...], h], axis=0)
    carry_ref[...] = h[seq_tile - MAX_WINDOW:, :]
    pos = si * seq_tile + lax.broadcasted_iota(jnp.int32, (seq_tile, 1), 0)
    for g, w in enumerate(POOL_WINDOWS):
        lo, hi = g * GROUP_DIM, (g + 1) * GROUP_DIM
        s = ext[:, lo:hi]
        span = 1
        while span < w:
            s = s + pltpu.roll(s, span, axis=0)
            span *= 2
        win = s[MAX_WINDOW:, :]
        cnt = jnp.minimum(pos + 1, w).astype(F32)
        m = win / cnt - h[:, lo:hi]
        y = _dot(m.astype(BF16), pw_ref[g])
        o_ref[0, :, lo:hi] = x[:, lo:hi] + y * ps_ref[:, lo:hi]


def _pool_layer(x, g, pw, ps):
    b, s, d = x.shape
    ts = POOL_SEQ_TILE
    return pl.pallas_call(
        functools.partial(_pool_kernel, seq_tile=ts),
        out_shape=jax.ShapeDtypeStruct(x.shape, x.dtype),
        grid=(b, s // ts),
        in_specs=[
            pl.BlockSpec((1, ts, d), lambda i, j: (i, j, 0)),
            pl.BlockSpec((1, d), lambda i, j: (0, 0)),
            pl.BlockSpec((N_GROUPS, GROUP_DIM, GROUP_DIM), lambda i, j: (0, 0, 0)),
            pl.BlockSpec((1, d), lambda i, j: (0, 0)),
        ],
        out_specs=pl.BlockSpec((1, ts, d), lambda i, j: (i, j, 0)),
        scratch_shapes=[pltpu.VMEM((MAX_WINDOW, d), F32)],
        compiler_params=pltpu.CompilerParams(
            dimension_semantics=("arbitrary", "arbitrary")),
        name="pool_mixer",
    )(x, g, pw, ps)


def _mlp_kernel(*refs, has_attn):
    if has_attn:
        x_ref, o_ref, wo_ref, g_ref, win_ref, wout_ref, out_ref = refs
        x = x_ref[...] + _dot(o_ref[...], wo_ref[...])
    else:
        x_ref, g_ref, win_ref, wout_ref, out_ref = refs
        x = x_ref[...]
    h = _rmsnorm(x, g_ref[...]).astype(BF16)
    acc = x
    for c in range(D_FF // FF_CHUNK):
        lo, hi = c * FF_CHUNK, (c + 1) * FF_CHUNK
        a = _dot(h, win_ref[:, lo:hi])
        a = jnp.square(jnp.maximum(a, 0.0)).astype(BF16)
        acc = acc + _dot(a, wout_ref[lo:hi, :])
    out_ref[...] = acc


def _resident(shape):
    zeros = (0,) * len(shape)
    return pl.BlockSpec(shape, lambda *_: zeros, pipeline_mode=pl.Buffered(1))


def _mlp_layer(x2, g, w_in, w_out, attn=None):
    n, d = x2.shape
    tm = TOKEN_TILE
    row = pl.BlockSpec((tm, d), lambda i: (i, 0))
    args, specs = [x2], [row]
    if attn is not None:
        o2, w_o = attn
        args += [o2, w_o]
        specs += [row, _resident(w_o.shape)]
    args += [g, w_in, w_out]
    specs += [_resident(g.shape), _resident(w_in.shape), _resident(w_out.shape)]
    return pl.pallas_call(
        functools.partial(_mlp_kernel, has_attn=attn is not None),
        out_shape=jax.ShapeDtypeStruct(x2.shape, x2.dtype),
        grid=(n // tm,),
        in_specs=specs,
        out_specs=row,
        compiler_params=pltpu.CompilerParams(
            dimension_semantics=("arbitrary",), vmem_limit_bytes=VMEM_LIMIT_BYTES),
        name="oproj_mlp" if attn is not None else "mlp",
    )(*args)


def _kv_kernel(x_ref, g_ref, wkt_ref, wv_ref, kn_ref, kt_ref, v_ref):
    h = _rmsnorm(x_ref[0], g_ref[...]).astype(BF16)
    kt = lax.dot_general(wkt_ref[...], h, (((1,), (1,)), ((), ())),
                         preferred_element_type=F32)
    t = kt.shape[-1]
    kt = kt.reshape(QK_WIDTH // HEAD_DIM, HEAD_DIM, t)
    ms = jnp.mean(kt * kt, axis=1, keepdims=True)
    kt = kt * lax.rsqrt(ms + EPS) * kn_ref[...][None]
    kt_ref[0] = kt.reshape(QK_WIDTH, t).astype(BF16)
    v_ref[0] = _dot(h, wv_ref[...]).astype(BF16)


def _kv_proj(x, g, w_kt, w_v, k_norm_col):
    b, s, d = x.shape
    tm = TOKEN_TILE
    return pl.pallas_call(
        _kv_kernel,
        out_shape=(jax.ShapeDtypeStruct((b, QK_WIDTH, s), BF16),
                   jax.ShapeDtypeStruct((b, s, w_v.shape[1]), BF16)),
        grid=(b, s // tm),
        in_specs=[
            pl.BlockSpec((1, tm, d), lambda i, j: (i, j, 0)),
            _resident(g.shape), _resident(w_kt.shape), _resident(w_v.shape),
            _resident(k_norm_col.shape),
        ],
        out_specs=(pl.BlockSpec((1, QK_WIDTH, tm), lambda i, j: (i, 0, j)),
                   pl.BlockSpec((1, tm, w_v.shape[1]), lambda i, j: (i, j, 0))),
        compiler_params=pltpu.CompilerParams(
            dimension_semantics=("arbitrary", "arbitrary")),
        name="kv_proj",
    )(x, g, w_kt, w_v, k_norm_col)


def _q_kernel(x_ref, g_ref, wq_ref, avg_ref, qg_ref, q_ref):
    h = _rmsnorm(x_ref[...], g_ref[...]).astype(BF16)
    q = _dot(h, wq_ref[...])
    ms = _dot((q * q).astype(BF16), avg_ref[...])
    q_ref[...] = (q * lax.rsqrt(ms + EPS) * qg_ref[...]).astype(BF16)


def _q_proj(x2, g, w_q, avg, q_gain):
    n, d = x2.shape
    tm = TOKEN_TILE
    return pl.pallas_call(
        _q_kernel,
        out_shape=jax.ShapeDtypeStruct((n, QK_WIDTH), BF16),
        grid=(n // tm,),
        in_specs=[pl.BlockSpec((tm, d), lambda i: (i, 0)),
                  _resident(g.shape), _resident(w_q.shape), _resident(avg.shape),
                  _resident(q_gain.shape)],
        out_specs=pl.BlockSpec((tm, QK_WIDTH), lambda i: (i, 0)),
        compiler_params=pltpu.CompilerParams(dimension_semantics=("arbitrary",)),
        name="q_proj",
    )(x2, g, w_q, avg, q_gain)


def _build_bias_tiles(rb_ref, bias_ref, t):
    row = lax.broadcasted_iota(jnp.int32, (t, t), 0)
    col = lax.broadcasted_iota(jnp.int32, (t, t), 1)
    for hh in range(N_HEADS):
        last = rb_ref[N_BUCKETS - 1, hh]
        for kind, offset in enumerate((0, t)):
            rel = row - col + offset
            tile = jnp.full((t, t), rb_ref[0, hh], F32)
            for bkt in range(1, N_BUCKETS):
                tile = jnp.where(rel >= BUCKET_START[bkt], rb_ref[bkt, hh], tile)
            tile = (tile - last) * LOG2E
            if offset == 0:
                tile = jnp.where(rel >= 0, tile, MASK_VALUE)
            bias_ref[hh, kind] = tile


def _attn_kernel(rb_ref, q_ref, kt_ref, v_ref, lam_ref, sub_ref, o_ref, bias_ref, vaug_ref,
                 *, lambda_init):
    t = ATTN_TILE
    seq = q_ref.shape[1]
    nq = seq // t
    hh = pl.program_id(1)

    @pl.when((pl.program_id(0) == 0) & (hh == 0))
    def _():
        _build_bias_tiles(rb_ref, bias_ref, t)

    vaug_ref[:, :V_DIM] = v_ref[0]
    lane_v = lax.broadcasted_iota(jnp.int32, (seq, V_DIM), 1)
    vaug_ref[:, V_DIM:] = jnp.where(lane_v == 0, 1.0, 0.0).astype(BF16)

    lam_p = lam_ref[...]
    lam = (jnp.exp(jnp.sum(lam_p[0:1] * lam_p[1:2], axis=-1, keepdims=True))
           - jnp.exp(jnp.sum(lam_p[2:3] * lam_p[3:4], axis=-1, keepdims=True))
           + lambda_init)
    lane_q = lax.broadcasted_iota(jnp.int32, (t, 2 * HEAD_DIM), 1)

    def block(qs, state, start, bias):
        kblk = kt_ref[0, :, pl.ds(start, t)]
        vblk = vaug_ref[pl.ds(start, t), :]
        new = []
        for c in range(2):
            s = _dot(qs[c], kblk)
            if bias is not None:
                s = s + bias
            row_max = jnp.max(s, axis=-1, keepdims=True)
            if state is None:
                m_new = row_max
            else:
                m_old, acc_old = state[c]
                m_new = jnp.maximum(m_old, row_max)
            p = jnp.exp2(s - m_new).astype(BF16)
            pv = _dot(p, vblk)
            if state is None:
                acc = pv
            else:
                acc = acc_old * jnp.exp2(m_old - m_new) + pv
            new.append((m_new, acc))
        return tuple(new)

    for qi in range(nq):
        q = q_ref[0, qi * t:(qi + 1) * t, :]
        zero = jnp.zeros_like(q)
        qs = (jnp.where(lane_q < HEAD_DIM, q, zero), jnp.where(lane_q >= HEAD_DIM, q, zero))
        state = None
        n_far = max(qi - 1, 0)
        if n_far >= 1:
            state = block(qs, None, 0, None)
        if n_far >= 2:
            def body(j, st):
                return block(qs, st, pl.multiple_of(j * t, t), None)
            state = lax.fori_loop(1, n_far, body, state)
        if qi >= 1:
            state = block(qs, state, (qi - 1) * t, bias_ref[hh, 1])
        state = block(qs, state, qi * t, bias_ref[hh, 0])
        outs = []
        for c in range(2):
            acc = state[c][1]
            outs.append(acc[:, :V_DIM] / acc[:, V_DIM:V_DIM + 1])
        o = outs[0] - lam * outs[1]
        o = _rmsnorm(o, sub_ref[...]) * (1.0 - lambda_init)
        o_ref[0, qi * t:(qi + 1) * t, :] = o.astype(BF16)


def _diff_attention(rel_bias, q, kt, v, lam_params, subln, lambda_init):
    b, s, _ = q.shape
    t = ATTN_TILE
    assert t > FAR_DISTANCE and s % t == 0
    return pl.pallas_call(
        functools.partial(_attn_kernel, lambda_init=lambda_init),
        out_shape=jax.ShapeDtypeStruct((b, s, N_HEADS * V_DIM), BF16),
        grid=(b, N_HEADS),
        in_specs=[
            pl.BlockSpec(memory_space=pltpu.SMEM),
            pl.BlockSpec((1, s, 2 * HEAD_DIM), lambda i, j: (i, 0, j)),
            pl.BlockSpec((1, 2 * HEAD_DIM, s), lambda i, j: (i, j, 0)),
            pl.BlockSpec((1, s, V_DIM), lambda i, j: (i, 0, j)),
            pl.BlockSpec(lam_params.shape, lambda i, j: (0, 0)),
            pl.BlockSpec(subln.shape, lambda i, j: (0, 0)),
        ],
        out_specs=pl.BlockSpec((1, s, V_DIM), lambda i, j: (i, 0, j)),
        scratch_shapes=[pltpu.VMEM((N_HEADS, 2, t, t), F32),
                        pltpu.VMEM((s, 2 * V_DIM), BF16)],
        compiler_params=pltpu.CompilerParams(
            dimension_semantics=("arbitrary", "arbitrary")),
        name="diff_attention",
    )(rel_bias, q, kt, v, lam_params, subln)


def kernel(x, norm_mix, norm_mlp, pool_w, pool_scale, kv_norm, w_kv, k_norm, rel_bias,
           w_q, q_norm, lam_q1, lam_k1, lam_q2, lam_k2, subln, w_o, w_mlp_in, w_mlp_out):
    b, s, d = x.shape
    n = b * s
    row = lambda a: a.reshape(1, -1).astype(F32)

    seg = np.arange(QK_WIDTH) // HEAD_DIM
    avg = jnp.asarray((seg[:, None] == seg[None, :]).astype(np.float32) / HEAD_DIM, dtype=BF16)

    for l in range(N_A):
        x = _pool_layer(x, row(norm_mix[l]), pool_w[l].astype(BF16), row(pool_scale[l]))
        x = _mlp_layer(x.reshape(n, d), row(norm_mlp[l]), w_mlp_in[l].astype(BF16),
                       w_mlp_out[l].astype(BF16)).reshape(b, s, d)

    kt, v = _kv_proj(x, row(kv_norm), w_kv[:, :QK_WIDTH].T.astype(BF16),
                     w_kv[:, QK_WIDTH:].astype(BF16), k_norm.reshape(HEAD_DIM, 1).astype(F32))

    x2 = x.reshape(n, d)
    for l in range(N_A, DEPTH):
        j = l - N_A
        lambda_init = 0.8 - 0.6 * math.exp(-0.3 * l)
        q_gain = row(jnp.tile(q_norm[j], QK_WIDTH // HEAD_DIM)) * (HEAD_DIM ** -0.5 * LOG2E)
        q = _q_proj(x2, row(norm_mix[l]), w_q[j].astype(BF16), avg, q_gain)
        lam_params = jnp.stack([lam_q1[j], lam_k1[j], lam_q2[j], lam_k2[j]]).astype(F32)
        o = _diff_attention(rel_bias.astype(F32), q.reshape(b, s, QK_WIDTH), kt, v,
                            lam_params, row(subln[j]), lambda_init)
        x2 = _mlp_layer(x2, row(norm_mlp[l]), w_mlp_in[l].astype(BF16),
                        w_mlp_out[l].astype(BF16),
                        attn=(o.reshape(n, N_HEADS * V_DIM), w_o[j].astype(BF16)))
    return x2.reshape(b, s, d)
```

```python
import functools
import math

import numpy as np
import jax
import jax.numpy as jnp
from jax import lax
from jax.experimental import pallas as pl
from jax.experimental.pallas import tpu as pltpu

D_MODEL = 1024
DEPTH = 4
N_A = DEPTH // 2
POOL_WINDOWS = (2, 4, 8, 16)
N_GROUPS = len(POOL_WINDOWS)
GROUP_DIM = D_MODEL // N_GROUPS
N_HEADS = 8
HEAD_DIM = D_MODEL // (2 * N_HEADS)
V_DIM = 2 * HEAD_DIM
QK_WIDTH = N_HEADS * 2 * HEAD_DIM
D_FF = 4 * D_MODEL
N_BUCKETS = 32
MAX_DISTANCE = 128
EPS = 1e-6

LOG2E = math.log2(math.e)
MASK_VALUE = -1e30
MAX_WINDOW = max(POOL_WINDOWS)
LANES = 128

VMEM_LIMIT_BYTES = 56 * 1024 * 1024

POOL_SEQ_TILE = 512
TOKEN_TILE = 512
FF_CHUNK = 1024
ATTN_TILE = 256

F32 = jnp.float32
BF16 = jnp.bfloat16


def _bucket_thresholds():
    max_exact = N_BUCKETS // 2
    n = np.arange(1, 4 * MAX_DISTANCE, dtype=np.float64)
    v = np.log(n / max_exact) / math.log(MAX_DISTANCE / max_exact) * (N_BUCKETS - max_exact)
    frac = v - np.floor(v)
    margin = np.minimum(frac, 1.0 - frac)[max_exact:MAX_DISTANCE - 1]
    assert margin.min() > 1e-3
    large = np.minimum(max_exact + v.astype(np.int64), N_BUCKETS - 1)
    bucket = np.where(n < max_exact, n.astype(np.int64), large)
    thr = [0] + [int(np.argmax(bucket >= b)) + 1 for b in range(1, N_BUCKETS)]
    return tuple(thr)


BUCKET_START = _bucket_thresholds()
FAR_DISTANCE = BUCKET_START[-1]


def _rmsnorm(xf, g):
    ms = jnp.mean(xf * xf, axis=-1, keepdims=True)
    return xf * lax.rsqrt(ms + EPS) * g


def _dot(a, b):
    return jnp.dot(a, b, preferred_element_type=F32)


def _pool_kernel(x_ref, g_ref, pw_ref, ps_ref, o_ref, carry_ref, *, seq_tile):
    si = pl.program_id(1)

    @pl.when(si == 0)
    def _():
        carry_ref[...] = jnp.zeros_like(carry_ref)

    x = x_ref[0]
    h = _rmsnorm(x, g_ref[...])
    ext = jnp.concatenate([carry_ref[...], h], axis=0)
    carry_ref[...] = h[seq_tile - MAX_WINDOW:, :]
    pos = si * seq_tile + lax.broadcasted_iota(jnp.int32, (seq_tile, 1), 0)
    for g, w in enumerate(POOL_WINDOWS):
        lo, hi = g * GROUP_DIM, (g + 1) * GROUP_DIM
        s = ext[:, lo:hi]
        span = 1
        while span < w:
            s = s + pltpu.roll(s, span, axis=0)
            span *= 2
        win = s[MAX_WINDOW:, :]
        cnt = jnp.minimum(pos + 1, w).astype(F32)
        m = win / cnt - h[:, lo:hi]
        y = _dot(m.astype(BF16), pw_ref[g])
        o_ref[0, :, lo:hi] = x[:, lo:hi] + y * ps_ref[:, lo:hi]


def _pool_layer(x, g, pw, ps):
    b, s, d = x.shape
    ts = POOL_SEQ_TILE
    return pl.pallas_call(
        functools.partial(_pool_kernel, seq_tile=ts),
        out_shape=jax.ShapeDtypeStruct(x.shape, x.dtype),
        grid=(b, s // ts),
        in_specs=[
            pl.BlockSpec((1, ts, d), lambda i, j: (i, j, 0)),
            pl.BlockSpec((1, d), lambda i, j: (0, 0)),
            pl.BlockSpec((N_GROUPS, GROUP_DIM, GROUP_DIM), lambda i, j: (0, 0, 0)),
            pl.BlockSpec((1, d), lambda i, j: (0, 0)),
        ],
        out_specs=pl.BlockSpec((1, ts, d), lambda i, j: (i, j, 0)),
        scratch_shapes=[pltpu.VMEM((MAX_WINDOW, d), F32)],
        compiler_params=pltpu.CompilerParams(
            dimension_semantics=("arbitrary", "arbitrary")),
        name="pool_mixer",
    )(x, g, pw, ps)


def _mlp_kernel(*refs, has_attn):
    if has_attn:
        x_ref, o_ref, wo_ref, g_ref, win_ref, wout_ref, out_ref = refs
        x = x_ref[...] + _dot(o_ref[...], wo_ref[...])
    else:
        x_ref, g_ref, win_ref, wout_ref, out_ref = refs
        x = x_ref[...]
    h = _rmsnorm(x, g_ref[...]).astype(BF16)
    acc = x
    for c in range(D_FF // FF_CHUNK):
        lo, hi = c * FF_CHUNK, (c + 1) * FF_CHUNK
        a = _dot(h, win_ref[:, lo:hi])
        a = jnp.square(jnp.maximum(a, 0.0)).astype(BF16)
        acc = acc + _dot(a, wout_ref[lo:hi, :])
    out_ref[...] = acc


def _resident(shape):
    zeros = (0,) * len(shape)
    return pl.BlockSpec(shape, lambda *_: zeros, pipeline_mode=pl.Buffered(1))


def _mlp_layer(x2, g, w_in, w_out, attn=None):
    n, d = x2.shape
    tm = TOKEN_TILE
    row = pl.BlockSpec((tm, d), lambda i: (i, 0))
    args, specs = [x2], [row]
    if attn is not None:
        o2, w_o = attn
        args += [o2, w_o]
        specs += [row, _resident(w_o.shape)]
    args += [g, w_in, w_out]
    specs += [_resident(g.shape), _resident(w_in.shape), _resident(w_out.shape)]
    return pl.pallas_call(
        functools.partial(_mlp_kernel, has_attn=attn is not None),
        out_shape=jax.ShapeDtypeStruct(x2.shape, x2.dtype),
        grid=(n // tm,),
        in_specs=specs,
        out_specs=row,
        compiler_params=pltpu.CompilerParams(
            dimension_semantics=("arbitrary",), vmem_limit_bytes=VMEM_LIMIT_BYTES),
        name="oproj_mlp" if attn is not None else "mlp",
    )(*args)


def _kv_kernel(x_ref, g_ref, wkt_ref, wv_ref, kn_ref, kt_ref, v_ref):
    h = _rmsnorm(x_ref[0], g_ref[...]).astype(BF16)
    kt = lax.dot_general(wkt_ref[...], h, (((1,), (1,)), ((), ())),
                         preferred_element_type=F32)
    t = kt.shape[-1]
    kt = kt.reshape(QK_WIDTH // HEAD_DIM, HEAD_DIM, t)
    ms = jnp.mean(kt * kt, axis=1, keepdims=True)
    kt = kt * lax.rsqrt(ms + EPS) * kn_ref[...][None]
    kt_ref[0] = kt.reshape(QK_WIDTH, t).astype(BF16)
    v_ref[0] = _dot(h, wv_ref[...]).astype(BF16)


def _kv_proj(x, g, w_kt, w_v, k_norm_col):
    b, s, d = x.shape
    tm = TOKEN_TILE
    return pl.pallas_call(
        _kv_kernel,
        out_shape=(jax.ShapeDtypeStruct((b, QK_WIDTH, s), BF16),
                   jax.ShapeDtypeStruct((b, s, w_v.shape[1]), BF16)),
        grid=(b, s // tm),
        in_specs=[
            pl.BlockSpec((1, tm, d), lambda i, j: (i, j, 0)),
            _resident(g.shape), _resident(w_kt.shape), _resident(w_v.shape),
            _resident(k_norm_col.shape),
        ],
        out_specs=(pl.BlockSpec((1, QK_WIDTH, tm), lambda i, j: (i, 0, j)),
                   pl.BlockSpec((1, tm, w_v.shape[1]), lambda i, j: (i, j, 0))),
        compiler_params=pltpu.CompilerParams(
            dimension_semantics=("arbitrary", "arbitrary")),
        name="kv_proj",
    )(x, g, w_kt, w_v, k_norm_col)


def _q_kernel(x_ref, g_ref, wq_ref, avg_ref, qg_ref, q_ref):
    h = _rmsnorm(x_ref[...], g_ref[...]).astype(BF16)
    q = _dot(h, wq_ref[...])
    ms = _dot((q * q).astype(BF16), avg_ref[...])
    q_ref[...] = (q * lax.rsqrt(ms + EPS) * qg_ref[...]).astype(BF16)


def _q_proj(x2, g, w_q, avg, q_gain):
    n, d = x2.shape
    tm = TOKEN_TILE
    return pl.pallas_call(
        _q_kernel,
        out_shape=jax.ShapeDtypeStruct((n, QK_WIDTH), BF16),
        grid=(n // tm,),
        in_specs=[pl.BlockSpec((tm, d), lambda i: (i, 0)),
                  _resident(g.shape), _resident(w_q.shape), _resident(avg.shape),
                  _resident(q_gain.shape)],
        out_specs=pl.BlockSpec((tm, QK_WIDTH), lambda i: (i, 0)),
        compiler_params=pltpu.CompilerParams(dimension_semantics=("arbitrary",)),
        name="q_proj",
    )(x2, g, w_q, avg, q_gain)


def _build_bias_tiles(rb_ref, bias_ref, t):
    row = lax.broadcasted_iota(jnp.int32, (t, t), 0)
    col = lax.broadcasted_iota(jnp.int32, (t, t), 1)
    for hh in range(N_HEADS):
        last = rb_ref[N_BUCKETS - 1, hh]
        for kind, offset in enumerate((0, t)):
            rel = row - col + offset
            tile = jnp.full((t, t), rb_ref[0, hh], F32)
            for bkt in range(1, N_BUCKETS):
                tile = jnp.where(rel >= BUCKET_START[bkt], rb_ref[bkt, hh], tile)
            tile = (tile - last) * LOG2E
            if offset == 0:
                tile = jnp.where(rel >= 0, tile, MASK_VALUE)
            bias_ref[hh, kind] = tile


def _attn_kernel(rb_ref, q_ref, kt_ref, v_ref, lam_ref, sub_ref, o_ref, bias_ref, vaug_ref,
                 m_ref, acc_ref, *, lambda_init):
    t = ATTN_TILE
    seq = q_ref.shape[1]
    nq = seq // t
    hh = pl.program_id(1)

    @pl.when((pl.program_id(0) == 0) & (hh == 0))
    def _():
        _build_bias_tiles(rb_ref, bias_ref, t)

    vaug_ref[:, :V_DIM] = v_ref[0]
    lane_v = lax.broadcasted_iota(jnp.int32, (seq, V_DIM), 1)
    vaug_ref[:, V_DIM:] = jnp.where(lane_v == 0, 1.0, 0.0).astype(BF16)

    lam_p = lam_ref[...]
    lam = (jnp.exp(jnp.sum(lam_p[0:1] * lam_p[1:2], axis=-1, keepdims=True))
           - jnp.exp(jnp.sum(lam_p[2:3] * lam_p[3:4], axis=-1, keepdims=True))
           + lambda_init)
    lane_q = lax.broadcasted_iota(jnp.int32, (seq, 2 * HEAD_DIM), 1)
    q = q_ref[0]
    zero = jnp.zeros_like(q)
    qs = (jnp.where(lane_q < HEAD_DIM, q, zero), jnp.where(lane_q >= HEAD_DIM, q, zero))

    for j in range(nq):
        r0 = j * t
        kblk = kt_ref[0, :, r0:r0 + t]
        vblk = vaug_ref[r0:r0 + t, :]
        for c in range(2):
            s = _dot(qs[c][r0:], kblk)
            pieces = [s[:t] + bias_ref[hh, 0]]
            if j + 1 < nq:
                pieces.append(s[t:2 * t] + bias_ref[hh, 1])
            if j + 2 < nq:
                pieces.append(s[2 * t:])
            s = jnp.concatenate(pieces, axis=0) if len(pieces) > 1 else pieces[0]
            row_max = jnp.max(s, axis=-1, keepdims=True)
            if j == 0:
                m_new = jnp.broadcast_to(row_max, (seq - r0, LANES))
            else:
                m_old = m_ref[c, r0:, :]
                m_new = jnp.maximum(m_old, row_max)
            p = jnp.concatenate(
                [jnp.exp2(s[:, k:k + LANES] - m_new) for k in range(0, t, LANES)], axis=1)
            pv = _dot(p.astype(BF16), vblk)
            if j == 0:
                acc = pv
            else:
                alpha = jnp.exp2(m_old - m_new)
                acc_old = acc_ref[c, r0:, :]
                acc = jnp.concatenate(
                    [acc_old[:, k:k + LANES] * alpha for k in range(0, 2 * V_DIM, LANES)],
                    axis=1) + pv
            acc_ref[c, r0:, :] = acc
            m_ref[c, r0:, :] = m_new

    outs = []
    for c in range(2):
        acc = acc_ref[c]
        outs.append(acc[:, :V_DIM] / acc[:, V_DIM:V_DIM + 1])
    o = outs[0] - lam * outs[1]
    o = _rmsnorm(o, sub_ref[...]) * (1.0 - lambda_init)
    o_ref[0] = o.astype(BF16)


def _diff_attention(rel_bias, q, kt, v, lam_params, subln, lambda_init):
    b, s, _ = q.shape
    t = ATTN_TILE
    assert t > FAR_DISTANCE and s % t == 0
    return pl.pallas_call(
        functools.partial(_attn_kernel, lambda_init=lambda_init),
        out_shape=jax.ShapeDtypeStruct((b, s, N_HEADS * V_DIM), BF16),
        grid=(b, N_HEADS),
        in_specs=[
            pl.BlockSpec(memory_space=pltpu.SMEM),
            pl.BlockSpec((1, s, 2 * HEAD_DIM), lambda i, j: (i, 0, j)),
            pl.BlockSpec((1, 2 * HEAD_DIM, s), lambda i, j: (i, j, 0)),
            pl.BlockSpec((1, s, V_DIM), lambda i, j: (i, 0, j)),
            pl.BlockSpec(lam_params.shape, lambda i, j: (0, 0)),
            pl.BlockSpec(subln.shape, lambda i, j: (0, 0)),
        ],
        out_specs=pl.BlockSpec((1, s, V_DIM), lambda i, j: (i, 0, j)),
        scratch_shapes=[pltpu.VMEM((N_HEADS, 2, t, t), F32),
                        pltpu.VMEM((s, 2 * V_DIM), BF16),
                        pltpu.VMEM((2, s, LANES), F32),
                        pltpu.VMEM((2, s, 2 * V_DIM), F32)],
        compiler_params=pltpu.CompilerParams(
            dimension_semantics=("arbitrary", "arbitrary"), vmem_limit_bytes=VMEM_LIMIT_BYTES),
        name="diff_attention",
    )(rel_bias, q, kt, v, lam_params, subln)


def kernel(x, norm_mix, norm_mlp, pool_w, pool_scale, kv_norm, w_kv, k_norm, rel_bias,
           w_q, q_norm, lam_q1, lam_k1, lam_q2, lam_k2, subln, w_o, w_mlp_in, w_mlp_out):
    b, s, d = x.shape
    n = b * s
    row = lambda a: a.reshape(1, -1).astype(F32)

    seg = np.arange(QK_WIDTH) // HEAD_DIM
    avg = jnp.asarray((seg[:, None] == seg[None, :]).astype(np.float32) / HEAD_DIM, dtype=BF16)

    for l in range(N_A):
        x = _pool_layer(x, row(norm_mix[l]), pool_w[l].astype(BF16), row(pool_scale[l]))
        x = _mlp_layer(x.reshape(n, d), row(norm_mlp[l]), w_mlp_in[l].astype(BF16),
                       w_mlp_out[l].astype(BF16)).reshape(b, s, d)

    kt, v = _kv_proj(x, row(kv_norm), w_kv[:, :QK_WIDTH].T.astype(BF16),
                     w_kv[:, QK_WIDTH:].astype(BF16), k_norm.reshape(HEAD_DIM, 1).astype(F32))

    x2 = x.reshape(n, d)
    for l in range(N_A, DEPTH):
        j = l - N_A
        lambda_init = 0.8 - 0.6 * math.exp(-0.3 * l)
        q_gain = row(jnp.tile(q_norm[j], QK_WIDTH // HEAD_DIM)) * (HEAD_DIM ** -0.5 * LOG2E)
        q = _q_proj(x2, row(norm_mix[l]), w_q[j].astype(BF16), avg, q_gain)
        lam_params = jnp.stack([lam_q1[j], lam_k1[j], lam_q2[j], lam_k2[j]]).astype(F32)
        o = _diff_attention(rel_bias.astype(F32), q.reshape(b, s, QK_WIDTH), kt, v,
                            lam_params, row(subln[j]), lambda_init)
        x2 = _mlp_layer(x2, row(norm_mlp[l]), w_mlp_in[l].astype(BF16),
                        w_mlp_out[l].astype(BF16),
                        attn=(o.reshape(n, N_HEADS * V_DIM), w_o[j].astype(BF16)))
    return x2.reshape(b, s, d)
```

```python
import functools
import math

import numpy as np
import jax
import jax.numpy as jnp
from jax import lax
from jax.experimental import pallas as pl
from jax.experimental.pallas import tpu as pltpu

D_MODEL = 1024
DEPTH = 4
N_A = DEPTH // 2
POOL_WINDOWS = (2, 4, 8, 16)
N_GROUPS = len(POOL_WINDOWS)
GROUP_DIM = D_MODEL // N_GROUPS
N_HEADS = 8
HEAD_DIM = D_MODEL // (2 * N_HEADS)
V_DIM = 2 * HEAD_DIM
QK_WIDTH = N_HEADS * 2 * HEAD_DIM
D_FF = 4 * D_MODEL
N_BUCKETS = 32
MAX_DISTANCE = 128
EPS = 1e-6

LOG2E = math.log2(math.e)
MASK_VALUE = -1e30
MAX_WINDOW = max(POOL_WINDOWS)
LANES = 128

VMEM_LIMIT_BYTES = 56 * 1024 * 1024

POOL_SEQ_TILE = 512
TOKEN_TILE = 512
FF_CHUNK = 1024
ATTN_TILE = 256

F32 = jnp.float32
BF16 = jnp.bfloat16


def _bucket_thresholds():
    max_exact = N_BUCKETS // 2
    n = np.arange(1, 4 * MAX_DISTANCE, dtype=np.float64)
    v = np.log(n / max_exact) / math.log(MAX_DISTANCE / max_exact) * (N_BUCKETS - max_exact)
    frac = v - np.floor(v)
    margin = np.minimum(frac, 1.0 - frac)[max_exact:MAX_DISTANCE - 1]
    assert margin.min() > 1e-3
    large = np.minimum(max_exact + v.astype(np.int64), N_BUCKETS - 1)
    bucket = np.where(n < max_exact, n.astype(np.int64), large)
    thr = [0] + [int(np.argmax(bucket >= b)) + 1 for b in range(1, N_BUCKETS)]
    return tuple(thr)


BUCKET_START = _bucket_thresholds()
FAR_DISTANCE = BUCKET_START[-1]


def _rmsnorm(xf, g):
    ms = jnp.mean(xf * xf, axis=-1, keepdims=True)
    return xf * lax.rsqrt(ms + EPS) * g


def _dot(a, b):
    return jnp.dot(a, b, preferred_element_type=F32)


def _pool_kernel(x_ref, g_ref, pw_ref, ps_ref, o_ref, carry_ref, *, seq_tile):
    si = pl.program_id(1)

    @pl.when(si == 0)
    def _():
        carry_ref[...] = jnp.zeros_like(carry_ref)

    x = x_ref[0]
    h = _rmsnorm(x, g_ref[...])
    ext = jnp.concatenate([carry_ref[...], h], axis=0)
    carry_ref[...] = h[seq_tile - MAX_WINDOW:, :]
    pos = si * seq_tile + lax.broadcasted_iota(jnp.int32, (seq_tile, 1), 0)
    for g, w in enumerate(POOL_WINDOWS):
        lo, hi = g * GROUP_DIM, (g + 1) * GROUP_DIM
        s = ext[:, lo:hi]
        span = 1
        while span < w:
            s = s + pltpu.roll(s, span, axis=0)
            span *= 2
        win = s[MAX_WINDOW:, :]
        cnt = jnp.minimum(pos + 1, w).astype(F32)
        m = win / cnt - h[:, lo:hi]
        y = _dot(m.astype(BF16), pw_ref[g])
        o_ref[0, :, lo:hi] = x[:, lo:hi] + y * ps_ref[:, lo:hi]


def _pool_layer(x, g, pw, ps):
    b, s, d = x.shape
    ts = POOL_SEQ_TILE
    return pl.pallas_call(
        functools.partial(_pool_kernel, seq_tile=ts),
        out_shape=jax.ShapeDtypeStruct(x.shape, x.dtype),
        grid=(b, s // ts),
        in_specs=[
            pl.BlockSpec((1, ts, d), lambda i, j: (i, j, 0)),
            pl.BlockSpec((1, d), lambda i, j: (0, 0)),
            pl.BlockSpec((N_GROUPS, GROUP_DIM, GROUP_DIM), lambda i, j: (0, 0, 0)),
            pl.BlockSpec((1, d), lambda i, j: (0, 0)),
        ],
        out_specs=pl.BlockSpec((1, ts, d), lambda i, j: (i, j, 0)),
        scratch_shapes=[pltpu.VMEM((MAX_WINDOW, d), F32)],
        compiler_params=pltpu.CompilerParams(
            dimension_semantics=("arbitrary", "arbitrary")),
        name="pool_mixer",
    )(x, g, pw, ps)


def _mlp_kernel(*refs, has_attn):
    if has_attn:
        x_ref, o_ref, wo_ref, g_ref, win_ref, wout_ref, out_ref = refs
        x = x_ref[...] + _dot(o_ref[...], wo_ref[...])
    else:
        x_ref, g_ref, win_ref, wout_ref, out_ref = refs
        x = x_ref[...]
    h = _rmsnorm(x, g_ref[...]).astype(BF16)
    acc = x
    for c in range(D_FF // FF_CHUNK):
        lo, hi = c * FF_CHUNK, (c + 1) * FF_CHUNK
        a = _dot(h, win_ref[:, lo:hi])
        a = jnp.square(jnp.maximum(a, 0.0)).astype(BF16)
        acc = acc + _dot(a, wout_ref[lo:hi, :])
    out_ref[...] = acc


def _resident(shape):
    zeros = (0,) * len(shape)
    return pl.BlockSpec(shape, lambda *_: zeros, pipeline_mode=pl.Buffered(1))


def _mlp_layer(x2, g, w_in, w_out, attn=None):
    n, d = x2.shape
    tm = TOKEN_TILE
    row = pl.BlockSpec((tm, d), lambda i: (i, 0))
    args, specs = [x2], [row]
    if attn is not None:
        o2, w_o = attn
        args += [o2, w_o]
        specs += [row, _resident(w_o.shape)]
    args += [g, w_in, w_out]
    specs += [_resident(g.shape), _resident(w_in.shape), _resident(w_out.shape)]
    return pl.pallas_call(
        functools.partial(_mlp_kernel, has_attn=attn is not None),
        out_shape=jax.ShapeDtypeStruct(x2.shape, x2.dtype),
        grid=(n // tm,),
        in_specs=specs,
        out_specs=row,
        compiler_params=pltpu.CompilerParams(
            dimension_semantics=("arbitrary",), vmem_limit_bytes=VMEM_LIMIT_BYTES),
        name="oproj_mlp" if attn is not None else "mlp",
    )(*args)


def _kv_kernel(x_ref, g_ref, wkt_ref, wv_ref, kn_ref, kt_ref, v_ref):
    h = _rmsnorm(x_ref[0], g_ref[...]).astype(BF16)
    kt = lax.dot_general(wkt_ref[...], h, (((1,), (1,)), ((), ())),
                         preferred_element_type=F32)
    t = kt.shape[-1]
    kt = kt.reshape(QK_WIDTH // HEAD_DIM, HEAD_DIM, t)
    ms = jnp.mean(kt * kt, axis=1, keepdims=True)
    kt = kt * lax.rsqrt(ms + EPS) * kn_ref[...][None]
    kt_ref[0] = kt.reshape(QK_WIDTH, t).astype(BF16)
    v_ref[0] = _dot(h, wv_ref[...]).astype(BF16)


def _kv_proj(x, g, w_kt, w_v, k_norm_col):
    b, s, d = x.shape
    tm = TOKEN_TILE
    return pl.pallas_call(
        _kv_kernel,
        out_shape=(jax.ShapeDtypeStruct((b, QK_WIDTH, s), BF16),
                   jax.ShapeDtypeStruct((b, s, w_v.shape[1]), BF16)),
        grid=(b, s // tm),
        in_specs=[
            pl.BlockSpec((1, tm, d), lambda i, j: (i, j, 0)),
            _resident(g.shape), _resident(w_kt.shape), _resident(w_v.shape),
            _resident(k_norm_col.shape),
        ],
        out_specs=(pl.BlockSpec((1, QK_WIDTH, tm), lambda i, j: (i, 0, j)),
                   pl.BlockSpec((1, tm, w_v.shape[1]), lambda i, j: (i, j, 0))),
        compiler_params=pltpu.CompilerParams(
            dimension_semantics=("arbitrary", "arbitrary")),
        name="kv_proj",
    )(x, g, w_kt, w_v, k_norm_col)


def _q_kernel(x_ref, g_ref, wq_ref, avg_ref, qg_ref, q_ref):
    h = _rmsnorm(x_ref[...], g_ref[...]).astype(BF16)
    q = _dot(h, wq_ref[...])
    ms = _dot((q * q).astype(BF16), avg_ref[...])
    q_ref[...] = (q * lax.rsqrt(ms + EPS) * qg_ref[...]).astype(BF16)


def _q_proj(x2, g, w_q, avg, q_gain):
    n, d = x2.shape
    tm = TOKEN_TILE
    return pl.pallas_call(
        _q_kernel,
        out_shape=jax.ShapeDtypeStruct((n, QK_WIDTH), BF16),
        grid=(n // tm,),
        in_specs=[pl.BlockSpec((tm, d), lambda i: (i, 0)),
                  _resident(g.shape), _resident(w_q.shape), _resident(avg.shape),
                  _resident(q_gain.shape)],
        out_specs=pl.BlockSpec((tm, QK_WIDTH), lambda i: (i, 0)),
        compiler_params=pltpu.CompilerParams(dimension_semantics=("arbitrary",)),
        name="q_proj",
    )(x2, g, w_q, avg, q_gain)


def _build_bias_tiles(rb_ref, bias_ref, t):
    row = lax.broadcasted_iota(jnp.int32, (t, t), 0)
    col = lax.broadcasted_iota(jnp.int32, (t, t), 1)
    for hh in range(N_HEADS):
        last = rb_ref[N_BUCKETS - 1, hh]
        for kind, offset in enumerate((0, t)):
            rel = row - col + offset
            tile = jnp.full((t, t), rb_ref[0, hh], F32)
            for bkt in range(1, N_BUCKETS):
                tile = jnp.where(rel >= BUCKET_START[bkt], rb_ref[bkt, hh], tile)
            tile = (tile - last) * LOG2E
            if offset == 0:
                tile = jnp.where(rel >= 0, tile, MASK_VALUE)
            bias_ref[hh, kind] = tile


def _attn_kernel(rb_ref, q_ref, kt_ref, v_ref, lam_ref, sub_ref, o_ref, bias_ref, vaug_ref,
                 *, lambda_init):
    t = ATTN_TILE
    seq = q_ref.shape[1]
    nq = seq // t
    hh = pl.program_id(1)

    @pl.when((pl.program_id(0) == 0) & (hh == 0))
    def _():
        _build_bias_tiles(rb_ref, bias_ref, t)

    vaug_ref[:, :V_DIM] = v_ref[0]
    lane_v = lax.broadcasted_iota(jnp.int32, (seq, V_DIM), 1)
    vaug_ref[:, V_DIM:] = jnp.where(lane_v == 0, 1.0, 0.0).astype(BF16)

    lam_p = lam_ref[...]
    lam = (jnp.exp(jnp.sum(lam_p[0:1] * lam_p[1:2], axis=-1, keepdims=True))
           - jnp.exp(jnp.sum(lam_p[2:3] * lam_p[3:4], axis=-1, keepdims=True))
           + lambda_init)
    lane_q = lax.broadcasted_iota(jnp.int32, (t, 2 * HEAD_DIM), 1)

    order = [x for pair in zip(reversed(range(nq)), range(nq)) for x in pair][:nq]
    for i in order:
        r0, kend = i * t, (i + 1) * t
        q = q_ref[0, r0:kend, :]
        zero = jnp.zeros_like(q)
        q_stack = jnp.concatenate([jnp.where(lane_q < HEAD_DIM, q, zero),
                                   jnp.where(lane_q >= HEAD_DIM, q, zero)], axis=0)
        s = _dot(q_stack, kt_ref[0, :, :kend])
        ps = []
        for c in range(2):
            cols = [s[c * t:(c + 1) * t, kb * t:(kb + 1) * t] for kb in range(i + 1)]
            cols[i] = cols[i] + bias_ref[hh, 0]
            if i >= 1:
                cols[i - 1] = cols[i - 1] + bias_ref[hh, 1]
            sc = jnp.concatenate(cols, axis=1) if i >= 1 else cols[0]
            m = jnp.max(sc, axis=-1, keepdims=True)
            ps.append(jnp.exp2(sc - m).astype(BF16))
        acc = _dot(jnp.concatenate(ps, axis=0), vaug_ref[:kend, :])
        outs = [acc[c * t:(c + 1) * t, :V_DIM] / acc[c * t:(c + 1) * t, V_DIM:V_DIM + 1]
                for c in range(2)]
        o = outs[0] - lam * outs[1]
        o = _rmsnorm(o, sub_ref[...]) * (1.0 - lambda_init)
        o_ref[0, r0:kend, :] = o.astype(BF16)


def _diff_attention(rel_bias, q, kt, v, lam_params, subln, lambda_init):
    b, s, _ = q.shape
    t = ATTN_TILE
    assert t > FAR_DISTANCE and s % t == 0
    return pl.pallas_call(
        functools.partial(_attn_kernel, lambda_init=lambda_init),
        out_shape=jax.ShapeDtypeStruct((b, s, N_HEADS * V_DIM), BF16),
        grid=(b, N_HEADS),
        in_specs=[
            pl.BlockSpec(memory_space=pltpu.SMEM),
            pl.BlockSpec((1, s, 2 * HEAD_DIM), lambda i, j: (i, 0, j)),
            pl.BlockSpec((1, 2 * HEAD_DIM, s), lambda i, j: (i, j, 0)),
            pl.BlockSpec((1, s, V_DIM), lambda i, j: (i, 0, j)),
            pl.BlockSpec(lam_params.shape, lambda i, j: (0, 0)),
            pl.BlockSpec(subln.shape, lambda i, j: (0, 0)),
        ],
        out_specs=pl.BlockSpec((1, s, V_DIM), lambda i, j: (i, 0, j)),
        scratch_shapes=[pltpu.VMEM((N_HEADS, 2, t, t), F32),
                        pltpu.VMEM((s, 2 * V_DIM), BF16)],
        compiler_params=pltpu.CompilerParams(
            dimension_semantics=("arbitrary", "arbitrary"), vmem_limit_bytes=VMEM_LIMIT_BYTES),
        name="diff_attention",
    )(rel_bias, q, kt, v, lam_params, subln)


def kernel(x, norm_mix, norm_mlp, pool_w, pool_scale, kv_norm, w_kv, k_norm, rel_bias,
           w_q, q_norm, lam_q1, lam_k1, lam_q2, lam_k2, subln, w_o, w_mlp_in, w_mlp_out):
    b, s, d = x.shape
    n = b * s
    row = lambda a: a.reshape(1, -1).astype(F32)

    seg = np.arange(QK_WIDTH) // HEAD_DIM
    avg = jnp.asarray((seg[:, None] == seg[None, :]).astype(np.float32) / HEAD_DIM, dtype=BF16)

    for l in range(N_A):
        x = _pool_layer(x, row(norm_mix[l]), pool_w[l].astype(BF16), row(pool_scale[l]))
        x = _mlp_layer(x.reshape(n, d), row(norm_mlp[l]), w_mlp_in[l].astype(BF16),
                       w_mlp_out[l].astype(BF16)).reshape(b, s, d)

    kt, v = _kv_proj(x, row(kv_norm), w_kv[:, :QK_WIDTH].T.astype(BF16),
                     w_kv[:, QK_WIDTH:].astype(BF16), k_norm.reshape(HEAD_DIM, 1).astype(F32))

    x2 = x.reshape(n, d)
    for l in range(N_A, DEPTH):
        j = l - N_A
        lambda_init = 0.8 - 0.6 * math.exp(-0.3 * l)
        q_gain = row(jnp.tile(q_norm[j], QK_WIDTH // HEAD_DIM)) * (HEAD_DIM ** -0.5 * LOG2E)
        q = _q_proj(x2, row(norm_mix[l]), w_q[j].astype(BF16), avg, q_gain)
        lam_params = jnp.stack([lam_q1[j], lam_k1[j], lam_q2[j], lam_k2[j]]).astype(F32)
        o = _diff_attention(rel_bias.astype(F32), q.reshape(b, s, QK_WIDTH), kt, v,
                            lam_params, row(subln[j]), lambda_init)
        x2 = _mlp_layer(x2, row(norm_mlp[l]), w_mlp_in[l].astype(BF16),
                        w_mlp_out[l].astype(BF16),
                        attn=(o.reshape(n, N_HEADS * V_DIM), w_o[j].astype(BF16)))
    return x2.reshape(b, s, d)
```

```python
import functools
import math

import numpy as np
import jax
import jax.numpy as jnp
from jax import lax
from jax.experimental import pallas as pl
from jax.experimental.pallas import tpu as pltpu

D_MODEL = 1024
DEPTH = 4
N_A = DEPTH // 2
POOL_WINDOWS = (2, 4, 8, 16)
N_GROUPS = len(POOL_WINDOWS)
GROUP_DIM = D_MODEL // N_GROUPS
N_HEADS = 8
HEAD_DIM = D_MODEL // (2 * N_HEADS)
V_DIM = 2 * HEAD_DIM
QK_WIDTH = N_HEADS * 2 * HEAD_DIM
D_FF = 4 * D_MODEL
N_BUCKETS = 32
MAX_DISTANCE = 128
EPS = 1e-6

LOG2E = math.log2(math.e)
MASK_VALUE = -1e30
MAX_WINDOW = max(POOL_WINDOWS)
LANES = 128

VMEM_LIMIT_BYTES = 56 * 1024 * 1024

POOL_SEQ_TILE = 512
TOKEN_TILE = 512
FF_CHUNK = 1024
ATTN_TILE = 256

F32 = jnp.float32
BF16 = jnp.bfloat16


def _bucket_thresholds():
    max_exact = N_BUCKETS // 2
    n = np.arange(1, 4 * MAX_DISTANCE, dtype=np.float64)
    v = np.log(n / max_exact) / math.log(MAX_DISTANCE / max_exact) * (N_BUCKETS - max_exact)
    frac = v - np.floor(v)
    margin = np.minimum(frac, 1.0 - frac)[max_exact:MAX_DISTANCE - 1]
    assert margin.min() > 1e-3
    large = np.minimum(max_exact + v.astype(np.int64), N_BUCKETS - 1)
    bucket = np.where(n < max_exact, n.astype(np.int64), large)
    thr = [0] + [int(np.argmax(bucket >= b)) + 1 for b in range(1, N_BUCKETS)]
    return tuple(thr)


BUCKET_START = _bucket_thresholds()
FAR_DISTANCE = BUCKET_START[-1]


def _rmsnorm(xf, g):
    ms = jnp.mean(xf * xf, axis=-1, keepdims=True)
    return xf * lax.rsqrt(ms + EPS) * g


def _dot(a, b):
    return jnp.dot(a, b, preferred_element_type=F32)


def _pool_kernel(x_ref, g_ref, pw_ref, ps_ref, o_ref, carry_ref, *, seq_tile):
    si = pl.program_id(1)

    @pl.when(si == 0)
    def _():
        carry_ref[...] = jnp.zeros_like(carry_ref)

    x = x_ref[0]
    h = _rmsnorm(x, g_ref[...])
    ext = jnp.concatenate([carry_ref[...], h], axis=0)
    carry_ref[...] = h[seq_tile - MAX_WINDOW:, :]
    pos = si * seq_tile + lax.broadcasted_iota(jnp.int32, (seq_tile, 1), 0)
    for g, w in enumerate(POOL_WINDOWS):
        lo, hi = g * GROUP_DIM, (g + 1) * GROUP_DIM
        s = ext[:, lo:hi]
        span = 1
        while span < w:
            s = s + pltpu.roll(s, span, axis=0)
            span *= 2
        win = s[MAX_WINDOW:, :]
        cnt = jnp.minimum(pos + 1, w).astype(F32)
        m = win / cnt - h[:, lo:hi]
        y = _dot(m.astype(BF16), pw_ref[g].astype(BF16))
        o_ref[0, :, lo:hi] = x[:, lo:hi] + y * ps_ref[:, lo:hi]


def _pool_layer(x, g, pw, ps, layer):
    b, s, d = x.shape
    ts = POOL_SEQ_TILE
    return pl.pallas_call(
        functools.partial(_pool_kernel, seq_tile=ts),
        out_shape=jax.ShapeDtypeStruct(x.shape, x.dtype),
        grid=(b, s // ts),
        in_specs=[
            pl.BlockSpec((1, ts, d), lambda i, j: (i, j, 0)),
            pl.BlockSpec((1, d), lambda i, j: (0, 0)),
            _resident_layer(pw, layer),
            pl.BlockSpec((1, d), lambda i, j: (0, 0)),
        ],
        out_specs=pl.BlockSpec((1, ts, d), lambda i, j: (i, j, 0)),
        scratch_shapes=[pltpu.VMEM((MAX_WINDOW, d), F32)],
        compiler_params=pltpu.CompilerParams(
            dimension_semantics=("arbitrary", "arbitrary")),
        name="pool_mixer",
    )(x, g, pw, ps)


def _mlp_kernel(*refs, has_attn):
    if has_attn:
        x_ref, o_ref, wo_ref, g_ref, win_ref, wout_ref, out_ref = refs
        x = x_ref[...] + _dot(o_ref[...], wo_ref[...].astype(BF16))
    else:
        x_ref, g_ref, win_ref, wout_ref, out_ref = refs
        x = x_ref[...]
    h = _rmsnorm(x, g_ref[...]).astype(BF16)
    acc = x
    for c in range(D_FF // FF_CHUNK):
        lo, hi = c * FF_CHUNK, (c + 1) * FF_CHUNK
        a = _dot(h, win_ref[:, lo:hi].astype(BF16))
        a = jnp.square(jnp.maximum(a, 0.0)).astype(BF16)
        acc = acc + _dot(a, wout_ref[lo:hi, :].astype(BF16))
    out_ref[...] = acc


def _resident(shape):
    zeros = (0,) * len(shape)
    return pl.BlockSpec(shape, lambda *_: zeros, pipeline_mode=pl.Buffered(1))


def _resident_layer(stacked, layer):
    index = (layer,) + (0,) * (stacked.ndim - 1)
    return pl.BlockSpec((None,) + stacked.shape[1:], lambda *_: index,
                        pipeline_mode=pl.Buffered(1))


def _mlp_layer(x2, g, w_in, w_out, layer, attn=None):
    n, d = x2.shape
    tm = TOKEN_TILE
    row = pl.BlockSpec((tm, d), lambda i: (i, 0))
    args, specs = [x2], [row]
    if attn is not None:
        o2, w_o, attn_layer = attn
        args += [o2, w_o]
        specs += [row, _resident_layer(w_o, attn_layer)]
    args += [g, w_in, w_out]
    specs += [_resident(g.shape), _resident_layer(w_in, layer), _resident_layer(w_out, layer)]
    return pl.pallas_call(
        functools.partial(_mlp_kernel, has_attn=attn is not None),
        out_shape=jax.ShapeDtypeStruct(x2.shape, x2.dtype),
        grid=(n // tm,),
        in_specs=specs,
        out_specs=row,
        compiler_params=pltpu.CompilerParams(
            dimension_semantics=("arbitrary",), vmem_limit_bytes=VMEM_LIMIT_BYTES),
        name="oproj_mlp" if attn is not None else "mlp",
    )(*args)


def _kv_kernel(x_ref, g_ref, wkv_ref, kn_ref, kt_ref, v_ref):
    h = _rmsnorm(x_ref[0], g_ref[...]).astype(BF16)
    w_k = wkv_ref[:, :QK_WIDTH].astype(BF16)
    kt = lax.dot_general(w_k, h, (((0,), (1,)), ((), ())),
                         preferred_element_type=F32)
    t = kt.shape[-1]
    kt = kt.reshape(QK_WIDTH // HEAD_DIM, HEAD_DIM, t)
    ms = jnp.mean(kt * kt, axis=1, keepdims=True)
    kt = kt * lax.rsqrt(ms + EPS) * kn_ref[...][None]
    kt_ref[0] = kt.reshape(QK_WIDTH, t).astype(BF16)
    v_ref[0] = _dot(h, wkv_ref[:, QK_WIDTH:].astype(BF16)).astype(BF16)


def _kv_proj(x, g, w_kv, k_norm_col):
    b, s, d = x.shape
    tm = TOKEN_TILE
    v_width = w_kv.shape[1] - QK_WIDTH
    return pl.pallas_call(
        _kv_kernel,
        out_shape=(jax.ShapeDtypeStruct((b, QK_WIDTH, s), BF16),
                   jax.ShapeDtypeStruct((b, s, v_width), BF16)),
        grid=(b, s // tm),
        in_specs=[
            pl.BlockSpec((1, tm, d), lambda i, j: (i, j, 0)),
            _resident(g.shape), _resident(w_kv.shape), _resident(k_norm_col.shape),
        ],
        out_specs=(pl.BlockSpec((1, QK_WIDTH, tm), lambda i, j: (i, 0, j)),
                   pl.BlockSpec((1, tm, v_width), lambda i, j: (i, j, 0))),
        compiler_params=pltpu.CompilerParams(
            dimension_semantics=("arbitrary", "arbitrary")),
        name="kv_proj",
    )(x, g, w_kv, k_norm_col)


def _q_kernel(x_ref, g_ref, wq_ref, avg_ref, qg_ref, q_ref):
    h = _rmsnorm(x_ref[...], g_ref[...]).astype(BF16)
    q = _dot(h, wq_ref[...].astype(BF16))
    ms = _dot((q * q).astype(BF16), avg_ref[...])
    q_ref[...] = (q * lax.rsqrt(ms + EPS) * qg_ref[...]).astype(BF16)


def _q_proj(x2, g, w_q, avg, q_gain, layer):
    n, d = x2.shape
    tm = TOKEN_TILE
    return pl.pallas_call(
        _q_kernel,
        out_shape=jax.ShapeDtypeStruct((n, QK_WIDTH), BF16),
        grid=(n // tm,),
        in_specs=[pl.BlockSpec((tm, d), lambda i: (i, 0)),
                  _resident(g.shape), _resident_layer(w_q, layer), _resident(avg.shape),
                  _resident(q_gain.shape)],
        out_specs=pl.BlockSpec((tm, QK_WIDTH), lambda i: (i, 0)),
        compiler_params=pltpu.CompilerParams(dimension_semantics=("arbitrary",)),
        name="q_proj",
    )(x2, g, w_q, avg, q_gain)


def _build_bias_tiles(rb_ref, bias_ref, t):
    row = lax.broadcasted_iota(jnp.int32, (t, t), 0)
    col = lax.broadcasted_iota(jnp.int32, (t, t), 1)
    for hh in range(N_HEADS):
        last = rb_ref[N_BUCKETS - 1, hh]
        for kind, offset in enumerate((0, t)):
            rel = row - col + offset
            tile = jnp.full((t, t), rb_ref[0, hh], F32)
            for bkt in range(1, N_BUCKETS):
                tile = jnp.where(rel >= BUCKET_START[bkt], rb_ref[bkt, hh], tile)
            tile = (tile - last) * LOG2E
            if offset == 0:
                tile = jnp.where(rel >= 0, tile, MASK_VALUE)
            bias_ref[hh, kind] = tile


def _attn_kernel(rb_ref, q_ref, kt_ref, v_ref, lam_ref, sub_ref, o_ref, bias_ref, vaug_ref,
                 *, lambda_init):
    t = ATTN_TILE
    seq = q_ref.shape[1]
    nq = seq // t
    hh = pl.program_id(1)

    @pl.when((pl.program_id(0) == 0) & (hh == 0))
    def _():
        _build_bias_tiles(rb_ref, bias_ref, t)

    vaug_ref[:, :V_DIM] = v_ref[0]
    lane_v = lax.broadcasted_iota(jnp.int32, (seq, V_DIM), 1)
    vaug_ref[:, V_DIM:] = jnp.where(lane_v == 0, 1.0, 0.0).astype(BF16)

    lam_p = lam_ref[...]
    lam = (jnp.exp(jnp.sum(lam_p[0:1] * lam_p[1:2], axis=-1, keepdims=True))
           - jnp.exp(jnp.sum(lam_p[2:3] * lam_p[3:4], axis=-1, keepdims=True))
           + lambda_init)
    lane_q = lax.broadcasted_iota(jnp.int32, (t, 2 * HEAD_DIM), 1)

    order = [x for pair in zip(reversed(range(nq)), range(nq)) for x in pair][:nq]
    for i in order:
        r0, kend = i * t, (i + 1) * t
        q = q_ref[0, r0:kend, :]
        zero = jnp.zeros_like(q)
        q_stack = jnp.concatenate([jnp.where(lane_q < HEAD_DIM, q, zero),
                                   jnp.where(lane_q >= HEAD_DIM, q, zero)], axis=0)
        s = _dot(q_stack, kt_ref[0, :, :kend])
        ps = []
        for c in range(2):
            cols = [s[c * t:(c + 1) * t, kb * t:(kb + 1) * t] for kb in range(i + 1)]
            cols[i] = cols[i] + bias_ref[hh, 0]
            if i >= 1:
                cols[i - 1] = cols[i - 1] + bias_ref[hh, 1]
            sc = jnp.concatenate(cols, axis=1) if i >= 1 else cols[0]
            m = jnp.max(sc, axis=-1, keepdims=True)
            ps.append(jnp.exp2(sc - m).astype(BF16))
        acc = _dot(jnp.concatenate(ps, axis=0), vaug_ref[:kend, :])
        outs = [acc[c * t:(c + 1) * t, :V_DIM] / acc[c * t:(c + 1) * t, V_DIM:V_DIM + 1]
                for c in range(2)]
        o = outs[0] - lam * outs[1]
        o = _rmsnorm(o, sub_ref[...]) * (1.0 - lambda_init)
        o_ref[0, r0:kend, :] = o.astype(BF16)


def _diff_attention(rel_bias, q, kt, v, lam_params, subln, lambda_init):
    b, s, _ = q.shape
    t = ATTN_TILE
    assert t > FAR_DISTANCE and s % t == 0
    return pl.pallas_call(
        functools.partial(_attn_kernel, lambda_init=lambda_init),
        out_shape=jax.ShapeDtypeStruct((b, s, N_HEADS * V_DIM), BF16),
        grid=(b, N_HEADS),
        in_specs=[
            pl.BlockSpec(memory_space=pltpu.SMEM),
            pl.BlockSpec((1, s, 2 * HEAD_DIM), lambda i, j: (i, 0, j)),
            pl.BlockSpec((1, 2 * HEAD_DIM, s), lambda i, j: (i, j, 0)),
            pl.BlockSpec((1, s, V_DIM), lambda i, j: (i, 0, j)),
            pl.BlockSpec(lam_params.shape, lambda i, j: (0, 0)),
            pl.BlockSpec(subln.shape, lambda i, j: (0, 0)),
        ],
        out_specs=pl.BlockSpec((1, s, V_DIM), lambda i, j: (i, 0, j)),
        scratch_shapes=[pltpu.VMEM((N_HEADS, 2, t, t), F32),
                        pltpu.VMEM((s, 2 * V_DIM), BF16)],
        compiler_params=pltpu.CompilerParams(
            dimension_semantics=("arbitrary", "arbitrary"), vmem_limit_bytes=VMEM_LIMIT_BYTES),
        name="diff_attention",
    )(rel_bias, q, kt, v, lam_params, subln)


def kernel(x, norm_mix, norm_mlp, pool_w, pool_scale, kv_norm, w_kv, k_norm, rel_bias,
           w_q, q_norm, lam_q1, lam_k1, lam_q2, lam_k2, subln, w_o, w_mlp_in, w_mlp_out):
    b, s, d = x.shape
    n = b * s
    row = lambda a: a.reshape(1, -1).astype(F32)

    seg = np.arange(QK_WIDTH) // HEAD_DIM
    avg = jnp.asarray((seg[:, None] == seg[None, :]).astype(np.float32) / HEAD_DIM, dtype=BF16)

    for l in range(N_A):
        x = _pool_layer(x, row(norm_mix[l]), pool_w, row(pool_scale[l]), l)
        x = _mlp_layer(x.reshape(n, d), row(norm_mlp[l]), w_mlp_in, w_mlp_out, l).reshape(b, s, d)

    kt, v = _kv_proj(x, row(kv_norm), w_kv, k_norm.reshape(HEAD_DIM, 1).astype(F32))

    x2 = x.reshape(n, d)
    for l in range(N_A, DEPTH):
        j = l - N_A
        lambda_init = 0.8 - 0.6 * math.exp(-0.3 * l)
        q_gain = row(jnp.tile(q_norm[j], QK_WIDTH // HEAD_DIM)) * (HEAD_DIM ** -0.5 * LOG2E)
        q = _q_proj(x2, row(norm_mix[l]), w_q, avg, q_gain, j)
        lam_params = jnp.stack([lam_q1[j], lam_k1[j], lam_q2[j], lam_k2[j]]).astype(F32)
        o = _diff_attention(rel_bias.astype(F32), q.reshape(b, s, QK_WIDTH), kt, v,
                            lam_params, row(subln[j]), lambda_init)
        x2 = _mlp_layer(x2, row(norm_mlp[l]), w_mlp_in, w_mlp_out, l,
                        attn=(o.reshape(n, N_HEADS * V_DIM), w_o, j))
    return x2.reshape(b, s, d)
```

```python
import functools
import math

import numpy as np
import jax
import jax.numpy as jnp
from jax import lax
from jax.experimental import pallas as pl
from jax.experimental.pallas import tpu as pltpu

D_MODEL = 1024
DEPTH = 4
N_A = DEPTH // 2
POOL_WINDOWS = (2, 4, 8, 16)
N_GROUPS = len(POOL_WINDOWS)
GROUP_DIM = D_MODEL // N_GROUPS
N_HEADS = 8
HEAD_DIM = D_MODEL // (2 * N_HEADS)
V_DIM = 2 * HEAD_DIM
QK_WIDTH = N_HEADS * 2 * HEAD_DIM
D_FF = 4 * D_MODEL
N_BUCKETS = 32
MAX_DISTANCE = 128
EPS = 1e-6

LOG2E = math.log2(math.e)
MASK_VALUE = -1e30
MAX_WINDOW = max(POOL_WINDOWS)
LANES = 128

VMEM_LIMIT_BYTES = 56 * 1024 * 1024

POOL_SEQ_TILE = 512
TOKEN_TILE = 512
FF_CHUNK = 1024
ATTN_TILE = 256
HEADS_PER_STEP = 2

F32 = jnp.float32
BF16 = jnp.bfloat16


def _bucket_thresholds():
    max_exact = N_BUCKETS // 2
    n = np.arange(1, 4 * MAX_DISTANCE, dtype=np.float64)
    v = np.log(n / max_exact) / math.log(MAX_DISTANCE / max_exact) * (N_BUCKETS - max_exact)
    frac = v - np.floor(v)
    margin = np.minimum(frac, 1.0 - frac)[max_exact:MAX_DISTANCE - 1]
    assert margin.min() > 1e-3
    large = np.minimum(max_exact + v.astype(np.int64), N_BUCKETS - 1)
    bucket = np.where(n < max_exact, n.astype(np.int64), large)
    thr = [0] + [int(np.argmax(bucket >= b)) + 1 for b in range(1, N_BUCKETS)]
    return tuple(thr)


BUCKET_START = _bucket_thresholds()
FAR_DISTANCE = BUCKET_START[-1]


def _rmsnorm(xf, g):
    ms = jnp.mean(xf * xf, axis=-1, keepdims=True)
    return xf * lax.rsqrt(ms + EPS) * g


def _dot(a, b):
    return jnp.dot(a, b, preferred_element_type=F32)


def _pool_prepare(x_ref, seq_tile_idx, g_ref, carry_ref, ext_ref):
    t = x_ref.shape[0]
    h = _rmsnorm(x_ref[...], g_ref[...])
    ext_ref[:MAX_WINDOW, :] = jnp.where(seq_tile_idx == 0, 0.0, carry_ref[...])
    ext_ref[MAX_WINDOW:, :] = h
    carry_ref[...] = h[t - MAX_WINDOW:, :]


def _pool_group(g, x_ref, seq_tile_idx, pw_ref, ps_ref, ext_ref, out_ref):
    t = x_ref.shape[0]
    w = POOL_WINDOWS[g]
    lo, hi = g * GROUP_DIM, (g + 1) * GROUP_DIM
    s = ext_ref[:, lo:hi]
    h = s[MAX_WINDOW:, :]
    span = 1
    while span < w:
        s = s + pltpu.roll(s, span, axis=0)
        span *= 2
    pos = seq_tile_idx * t + lax.broadcasted_iota(jnp.int32, (t, 1), 0)
    cnt = jnp.minimum(pos + 1, w).astype(F32)
    m = s[MAX_WINDOW:, :] / cnt - h
    y = _dot(m.astype(BF16), pw_ref[g].astype(BF16))
    out_ref[:, lo:hi] = x_ref[:, lo:hi] + y * ps_ref[:, lo:hi]


def _pool_mlp_kernel(x0_ref, xn_ref, gmix_ref, pw_ref, ps_ref, g_ref, win_ref, wout_ref, out_ref,
                     carry_ref, ext_ref, mixed_ref, *, tiles_per_seq):
    k = pl.program_id(0)
    last = pl.num_programs(0) - 1

    @pl.when(k == 0)
    def _():
        _pool_prepare(x0_ref, 0, gmix_ref, carry_ref, ext_ref)
        for g in range(N_GROUPS):
            _pool_group(g, x0_ref, 0, pw_ref, ps_ref, ext_ref, mixed_ref.at[0])

    nxt = jnp.minimum(k + 1, last) % tiles_per_seq
    _pool_prepare(xn_ref, nxt, gmix_ref, carry_ref, ext_ref)
    side_jobs = [functools.partial(_pool_group, g, xn_ref, nxt, pw_ref, ps_ref, ext_ref,
                                   mixed_ref.at[(k + 1) % 2]) for g in range(N_GROUPS)]
    out_ref[...] = _mlp(mixed_ref[k % 2], g_ref, win_ref, wout_ref, side_jobs)


def _pool_mlp_layer(x2, g_mix, pw, ps, g, w_in, w_out, layer, tiles_per_seq):
    n, d = x2.shape
    tm = TOKEN_TILE
    last = n // tm - 1
    return pl.pallas_call(
        functools.partial(_pool_mlp_kernel, tiles_per_seq=tiles_per_seq),
        out_shape=jax.ShapeDtypeStruct(x2.shape, x2.dtype),
        grid=(n // tm,),
        in_specs=[
            pl.BlockSpec((tm, d), lambda i: (0, 0), pipeline_mode=pl.Buffered(1)),
            pl.BlockSpec((tm, d), lambda i: (jnp.minimum(i + 1, last), 0)),
            _resident(g_mix.shape), _resident_layer(pw, layer), _resident(ps.shape),
            _resident(g.shape), _resident_layer(w_in, layer), _resident_layer(w_out, layer),
        ],
        out_specs=pl.BlockSpec((tm, d), lambda i: (i, 0)),
        scratch_shapes=[pltpu.VMEM((MAX_WINDOW, d), F32),
                        pltpu.VMEM((MAX_WINDOW + tm, d), F32),
                        pltpu.VMEM((2, tm, d), F32)],
        compiler_params=pltpu.CompilerParams(
            dimension_semantics=("arbitrary",), vmem_limit_bytes=VMEM_LIMIT_BYTES),
        name="pool_mlp",
    )(x2, x2, g_mix, pw, ps, g, w_in, w_out)


def _mlp(x, g_ref, win_ref, wout_ref, side_jobs=()):
    h = _rmsnorm(x, g_ref[...]).astype(BF16)
    acc = x
    for c in range(D_FF // FF_CHUNK):
        lo, hi = c * FF_CHUNK, (c + 1) * FF_CHUNK
        a = _dot(h, win_ref[:, lo:hi].astype(BF16))
        if c < len(side_jobs):
            side_jobs[c]()
        a = jnp.square(jnp.maximum(a, 0.0)).astype(BF16)
        acc = acc + _dot(a, wout_ref[lo:hi, :].astype(BF16))
    return acc


def _oproj_mlp_kernel(x_ref, o_ref, wo_ref, g_ref, win_ref, wout_ref, out_ref):
    x = x_ref[...] + _dot(o_ref[...], wo_ref[...].astype(BF16))
    out_ref[...] = _mlp(x, g_ref, win_ref, wout_ref)


def _resident(shape):
    zeros = (0,) * len(shape)
    return pl.BlockSpec(shape, lambda *_: zeros, pipeline_mode=pl.Buffered(1))


def _resident_layer(stacked, layer):
    index = (layer,) + (0,) * (stacked.ndim - 1)
    return pl.BlockSpec((None,) + stacked.shape[1:], lambda *_: index,
                        pipeline_mode=pl.Buffered(1))


def _oproj_mlp_layer(x2, o2, w_o, attn_layer, g, w_in, w_out, layer):
    n, d = x2.shape
    tm = TOKEN_TILE
    row = pl.BlockSpec((tm, d), lambda i: (i, 0))
    return pl.pallas_call(
        _oproj_mlp_kernel,
        out_shape=jax.ShapeDtypeStruct(x2.shape, x2.dtype),
        grid=(n // tm,),
        in_specs=[row, row, _resident_layer(w_o, attn_layer), _resident(g.shape),
                  _resident_layer(w_in, layer), _resident_layer(w_out, layer)],
        out_specs=row,
        compiler_params=pltpu.CompilerParams(
            dimension_semantics=("arbitrary",), vmem_limit_bytes=VMEM_LIMIT_BYTES),
        name="oproj_mlp",
    )(x2, o2, w_o, g, w_in, w_out)


def _kv_kernel(x_ref, g_ref, wkv_ref, kn_ref, kt_ref, v_ref):
    h = _rmsnorm(x_ref[0], g_ref[...]).astype(BF16)
    w_k = wkv_ref[:, :QK_WIDTH].astype(BF16)
    kt = lax.dot_general(w_k, h, (((0,), (1,)), ((), ())),
                         preferred_element_type=F32)
    t = kt.shape[-1]
    kt = kt.reshape(QK_WIDTH // HEAD_DIM, HEAD_DIM, t)
    ms = jnp.mean(kt * kt, axis=1, keepdims=True)
    kt = kt * lax.rsqrt(ms + EPS) * kn_ref[...][None]
    kt_ref[0] = kt.reshape(QK_WIDTH, t).astype(BF16)
    v_ref[0] = _dot(h, wkv_ref[:, QK_WIDTH:].astype(BF16)).astype(BF16)


def _kv_proj(x, g, w_kv, k_norm_col):
    b, s, d = x.shape
    tm = TOKEN_TILE
    v_width = w_kv.shape[1] - QK_WIDTH
    return pl.pallas_call(
        _kv_kernel,
        out_shape=(jax.ShapeDtypeStruct((b, QK_WIDTH, s), BF16),
                   jax.ShapeDtypeStruct((b, s, v_width), BF16)),
        grid=(b, s // tm),
        in_specs=[
            pl.BlockSpec((1, tm, d), lambda i, j: (i, j, 0)),
            _resident(g.shape), _resident(w_kv.shape), _resident(k_norm_col.shape),
        ],
        out_specs=(pl.BlockSpec((1, QK_WIDTH, tm), lambda i, j: (i, 0, j)),
                   pl.BlockSpec((1, tm, v_width), lambda i, j: (i, j, 0))),
        compiler_params=pltpu.CompilerParams(
            dimension_semantics=("arbitrary", "arbitrary")),
        name="kv_proj",
    )(x, g, w_kv, k_norm_col)


def _q_kernel(x_ref, g_ref, wq_ref, avg_ref, qg_ref, q_ref):
    h = _rmsnorm(x_ref[...], g_ref[...]).astype(BF16)
    q = _dot(h, wq_ref[...].astype(BF16))
    ms = _dot((q * q).astype(BF16), avg_ref[...])
    q_ref[...] = (q * lax.rsqrt(ms + EPS) * qg_ref[...]).astype(BF16)


def _q_proj(x2, g, w_q, avg, q_gain, layer):
    n, d = x2.shape
    tm = TOKEN_TILE
    return pl.pallas_call(
        _q_kernel,
        out_shape=jax.ShapeDtypeStruct((n, QK_WIDTH), BF16),
        grid=(n // tm,),
        in_specs=[pl.BlockSpec((tm, d), lambda i: (i, 0)),
                  _resident(g.shape), _resident_layer(w_q, layer), _resident(avg.shape),
                  _resident(q_gain.shape)],
        out_specs=pl.BlockSpec((tm, QK_WIDTH), lambda i: (i, 0)),
        compiler_params=pltpu.CompilerParams(dimension_semantics=("arbitrary",)),
        name="q_proj",
    )(x2, g, w_q, avg, q_gain)


def _build_bias_tiles(rb_ref, bias_ref, t):
    row = lax.broadcasted_iota(jnp.int32, (t, t), 0)
    col = lax.broadcasted_iota(jnp.int32, (t, t), 1)
    for hh in range(N_HEADS):
        last = rb_ref[N_BUCKETS - 1, hh]
        for kind, offset in enumerate((0, t)):
            rel = row - col + offset
            tile = jnp.full((t, t), rb_ref[0, hh], F32)
            for bkt in range(1, N_BUCKETS):
                tile = jnp.where(rel >= BUCKET_START[bkt], rb_ref[bkt, hh], tile)
            tile = (tile - last) * LOG2E
            if offset == 0:
                tile = jnp.where(rel >= 0, tile, MASK_VALUE)
            bias_ref[hh, kind] = tile


def _attn_kernel(rb_ref, q_ref, kt_ref, v_ref, lam_ref, sub_ref, o_ref, bias_ref, vaug_ref,
                 *, lambda_init):
    t = ATTN_TILE
    seq = q_ref.shape[1]
    nq = seq // t
    qk, vd = 2 * HEAD_DIM, V_DIM
    head0 = pl.program_id(1) * HEADS_PER_STEP

    @pl.when((pl.program_id(0) == 0) & (pl.program_id(1) == 0))
    def _():
        _build_bias_tiles(rb_ref, bias_ref, t)

    lane_v = lax.broadcasted_iota(jnp.int32, (seq, vd), 1)
    for a in range(HEADS_PER_STEP):
        vaug_ref[a, :, :vd] = v_ref[0, :, a * vd:(a + 1) * vd]
        vaug_ref[a, :, vd:] = jnp.where(lane_v == 0, 1.0, 0.0).astype(BF16)

    lam_p = lam_ref[...]
    lam = (jnp.exp(jnp.sum(lam_p[0:1] * lam_p[1:2], axis=-1, keepdims=True))
           - jnp.exp(jnp.sum(lam_p[2:3] * lam_p[3:4], axis=-1, keepdims=True))
           + lambda_init)
    lane_q = lax.broadcasted_iota(jnp.int32, (t, 2 * HEAD_DIM), 1)

    def scores(a, i):
        q = q_ref[0, i * t:(i + 1) * t, a * qk:(a + 1) * qk]
        zero = jnp.zeros_like(q)
        q_stack = jnp.concatenate([jnp.where(lane_q < HEAD_DIM, q, zero),
                                   jnp.where(lane_q >= HEAD_DIM, q, zero)], axis=0)
        return _dot(q_stack, kt_ref[0, a * qk:(a + 1) * qk, :(i + 1) * t])

    def probabilities(a, i, s):
        hh = head0 + a
        ps = []
        for c in range(2):
            cols = [s[c * t:(c + 1) * t, kb * t:(kb + 1) * t] for kb in range(i + 1)]
            cols[i] = cols[i] + bias_ref[hh, 0]
            if i >= 1:
                cols[i - 1] = cols[i - 1] + bias_ref[hh, 1]
            sc = jnp.concatenate(cols, axis=1) if i >= 1 else cols[0]
            m = jnp.max(sc, axis=-1, keepdims=True)
            ps.append(jnp.exp2(sc - m).astype(BF16))
        return jnp.concatenate(ps, axis=0)

    def output(a, i, p):
        acc = _dot(p, vaug_ref[a, :(i + 1) * t, :])
        outs = [acc[c * t:(c + 1) * t, :vd] / acc[c * t:(c + 1) * t, vd:vd + 1]
                for c in range(2)]
        o = outs[0] - lam * outs[1]
        o = _rmsnorm(o, sub_ref[...]) * (1.0 - lambda_init)
        o_ref[0, i * t:(i + 1) * t, a * vd:(a + 1) * vd] = o.astype(BF16)

    order = [1] + list(range(nq - 1, 1, -1)) + [0]
    assert sorted(order) == list(range(nq))
    blocks = [(a, i) for i in order for a in range(HEADS_PER_STEP)]
    s_cur = p_cur = None
    for n in range(len(blocks) + 2):
        p_next = probabilities(*blocks[n - 1], s_cur) if 1 <= n <= len(blocks) else None
        s_next = scores(*blocks[n]) if n < len(blocks) else None
        if n >= 2:
            output(*blocks[n - 2], p_cur)
        s_cur, p_cur = s_next, p_next


def _diff_attention(rel_bias, q, kt, v, lam_params, subln, lambda_init):
    b, s, _ = q.shape
    t = ATTN_TILE
    hps = HEADS_PER_STEP
    assert t > FAR_DISTANCE and s % t == 0 and N_HEADS % hps == 0
    return pl.pallas_call(
        functools.partial(_attn_kernel, lambda_init=lambda_init),
        out_shape=jax.ShapeDtypeStruct((b, s, N_HEADS * V_DIM), BF16),
        grid=(b, N_HEADS // hps),
        in_specs=[
            pl.BlockSpec(memory_space=pltpu.SMEM),
            pl.BlockSpec((1, s, hps * 2 * HEAD_DIM), lambda i, j: (i, 0, j)),
            pl.BlockSpec((1, hps * 2 * HEAD_DIM, s), lambda i, j: (i, j, 0)),
            pl.BlockSpec((1, s, hps * V_DIM), lambda i, j: (i, 0, j)),
            pl.BlockSpec(lam_params.shape, lambda i, j: (0, 0)),
            pl.BlockSpec(subln.shape, lambda i, j: (0, 0)),
        ],
        out_specs=pl.BlockSpec((1, s, hps * V_DIM), lambda i, j: (i, 0, j)),
        scratch_shapes=[pltpu.VMEM((N_HEADS, 2, t, t), F32),
                        pltpu.VMEM((hps, s, 2 * V_DIM), BF16)],
        compiler_params=pltpu.CompilerParams(
            dimension_semantics=("arbitrary", "arbitrary"), vmem_limit_bytes=VMEM_LIMIT_BYTES),
        name="diff_attention",
    )(rel_bias, q, kt, v, lam_params, subln)


def kernel(x, norm_mix, norm_mlp, pool_w, pool_scale, kv_norm, w_kv, k_norm, rel_bias,
           w_q, q_norm, lam_q1, lam_k1, lam_q2, lam_k2, subln, w_o, w_mlp_in, w_mlp_out):
    b, s, d = x.shape
    n = b * s
    row = lambda a: a.reshape(1, -1).astype(F32)

    seg = np.arange(QK_WIDTH) // HEAD_DIM
    avg = jnp.asarray((seg[:, None] == seg[None, :]).astype(np.float32) / HEAD_DIM, dtype=BF16)

    assert s % TOKEN_TILE == 0 and TOKEN_TILE >= MAX_WINDOW
    x2 = x.reshape(n, d)
    for l in range(N_A):
        x2 = _pool_mlp_layer(x2, row(norm_mix[l]), pool_w, row(pool_scale[l]), row(norm_mlp[l]),
                             w_mlp_in, w_mlp_out, l, s // TOKEN_TILE)

    kt, v = _kv_proj(x2.reshape(b, s, d), row(kv_norm), w_kv,
                     k_norm.reshape(HEAD_DIM, 1).astype(F32))

    for l in range(N_A, DEPTH):
        j = l - N_A
        lambda_init = 0.8 - 0.6 * math.exp(-0.3 * l)
        q_gain = row(jnp.tile(q_norm[j], QK_WIDTH // HEAD_DIM)) * (HEAD_DIM ** -0.5 * LOG2E)
        q = _q_proj(x2, row(norm_mix[l]), w_q, avg, q_gain, j)
        lam_params = jnp.stack([lam_q1[j], lam_k1[j], lam_q2[j], lam_k2[j]]).astype(F32)
        o = _diff_attention(rel_bias.astype(F32), q.reshape(b, s, QK_WIDTH), kt, v,
                            lam_params, row(subln[j]), lambda_init)
        x2 = _oproj_mlp_layer(x2, o.reshape(n, N_HEADS * V_DIM), w_o, j, row(norm_mlp[l]),
                              w_mlp_in, w_mlp_out, l)
    return x2.reshape(b, s, d)
```

```python
import functools
import math

import numpy as np
import jax
import jax.numpy as jnp
from jax import lax
from jax.experimental import pallas as pl
from jax.experimental.pallas import tpu as pltpu

D_MODEL = 1024
DEPTH = 4
N_A = DEPTH // 2
POOL_WINDOWS = (2, 4, 8, 16)
N_GROUPS = len(POOL_WINDOWS)
GROUP_DIM = D_MODEL // N_GROUPS
N_HEADS = 8
HEAD_DIM = D_MODEL // (2 * N_HEADS)
V_DIM = 2 * HEAD_DIM
QK_WIDTH = N_HEADS * 2 * HEAD_DIM
D_FF = 4 * D_MODEL
N_BUCKETS = 32
MAX_DISTANCE = 128
EPS = 1e-6

LOG2E = math.log2(math.e)
MASK_VALUE = -1e30
MAX_WINDOW = max(POOL_WINDOWS)
LANES = 128
MXU_TILE = 256

VMEM_LIMIT_BYTES = 56 * 1024 * 1024

TOKEN_TILE = 512
FF_CHUNK = 1024
ATTN_TILE = 256
HEADS_PER_STEP = 2

F32 = jnp.float32
BF16 = jnp.bfloat16


def _bucket_thresholds():
    max_exact = N_BUCKETS // 2
    n = np.arange(1, 4 * MAX_DISTANCE, dtype=np.float64)
    v = np.log(n / max_exact) / math.log(MAX_DISTANCE / max_exact) * (N_BUCKETS - max_exact)
    frac = v - np.floor(v)
    margin = np.minimum(frac, 1.0 - frac)[max_exact:MAX_DISTANCE - 1]
    assert margin.min() > 1e-3
    large = np.minimum(max_exact + v.astype(np.int64), N_BUCKETS - 1)
    bucket = np.where(n < max_exact, n.astype(np.int64), large)
    thr = [0] + [int(np.argmax(bucket >= b)) + 1 for b in range(1, N_BUCKETS)]
    return tuple(thr)


BUCKET_START = _bucket_thresholds()
FAR_DISTANCE = BUCKET_START[-1]


def _rmsnorm(xf, g):
    ms = jnp.mean(xf * xf, axis=-1, keepdims=True)
    return xf * lax.rsqrt(ms + EPS) * g


def _dot(a, b):
    return jnp.dot(a, b, preferred_element_type=F32)


def _pool_prepare(x_ref, seq_tile_idx, g_ref, carry_ref, ext_ref):
    t = x_ref.shape[0]
    h = _rmsnorm(x_ref[...], g_ref[...])
    ext_ref[:MAX_WINDOW, :] = jnp.where(seq_tile_idx == 0, 0.0, carry_ref[...])
    ext_ref[MAX_WINDOW:, :] = h
    carry_ref[...] = h[t - MAX_WINDOW:, :]


def _pool_group(g, x_ref, seq_tile_idx, pw_ref, ps_ref, ext_ref, out_ref):
    t = x_ref.shape[0]
    w = POOL_WINDOWS[g]
    lo, hi = g * GROUP_DIM, (g + 1) * GROUP_DIM
    s = ext_ref[:, lo:hi]
    h = s[MAX_WINDOW:, :]
    span = 1
    while span < w:
        s = s + pltpu.roll(s, span, axis=0)
        span *= 2
    pos = seq_tile_idx * t + lax.broadcasted_iota(jnp.int32, (t, 1), 0)
    cnt = jnp.minimum(pos + 1, w).astype(F32)
    m = s[MAX_WINDOW:, :] / cnt - h
    y = _dot(m.astype(BF16), pw_ref[g].astype(BF16))
    out_ref[:, lo:hi] = x_ref[:, lo:hi] + y * ps_ref[:, lo:hi]


def _pool_mlp_kernel(x0_ref, xn_ref, gmix_ref, pw_ref, ps_ref, g_ref, win_ref, wout_ref, out_ref,
                     carry_ref, ext_ref, mixed_ref, *, tiles_per_seq):
    k = pl.program_id(0)
    last = pl.num_programs(0) - 1

    @pl.when(k == 0)
    def _():
        _pool_prepare(x0_ref, 0, gmix_ref, carry_ref, ext_ref)
        for g in range(N_GROUPS):
            _pool_group(g, x0_ref, 0, pw_ref, ps_ref, ext_ref, mixed_ref.at[0])

    nxt = jnp.minimum(k + 1, last) % tiles_per_seq
    _pool_prepare(xn_ref, nxt, gmix_ref, carry_ref, ext_ref)
    side_jobs = [functools.partial(_pool_group, g, xn_ref, nxt, pw_ref, ps_ref, ext_ref,
                                   mixed_ref.at[(k + 1) % 2]) for g in range(N_GROUPS)]
    out_ref[...] = _mlp(mixed_ref[k % 2], g_ref, win_ref, wout_ref, side_jobs)


def _pool_mlp_layer(x2, g_mix, pw, ps, g, w_in, w_out, layer, tiles_per_seq):
    n, d = x2.shape
    tm = TOKEN_TILE
    last = n // tm - 1
    return pl.pallas_call(
        functools.partial(_pool_mlp_kernel, tiles_per_seq=tiles_per_seq),
        out_shape=jax.ShapeDtypeStruct(x2.shape, x2.dtype),
        grid=(n // tm,),
        in_specs=[
            pl.BlockSpec((tm, d), lambda i: (0, 0), pipeline_mode=pl.Buffered(1)),
            pl.BlockSpec((tm, d), lambda i: (jnp.minimum(i + 1, last), 0)),
            _resident(g_mix.shape), _resident_layer(pw, layer), _resident(ps.shape),
            _resident(g.shape), _resident_layer(w_in, layer), _resident_layer(w_out, layer),
        ],
        out_specs=pl.BlockSpec((tm, d), lambda i: (i, 0)),
        scratch_shapes=[pltpu.VMEM((MAX_WINDOW, d), F32),
                        pltpu.VMEM((MAX_WINDOW + tm, d), F32),
                        pltpu.VMEM((2, tm, d), F32)],
        compiler_params=pltpu.CompilerParams(
            dimension_semantics=("arbitrary",), vmem_limit_bytes=VMEM_LIMIT_BYTES),
        name="pool_mlp",
    )(x2, x2, g_mix, pw, ps, g, w_in, w_out)


def _mlp(x, g_ref, win_ref, wout_ref, side_jobs=()):
    h = _rmsnorm(x, g_ref[...]).astype(BF16)
    acc = x
    for c in range(D_FF // FF_CHUNK):
        lo, hi = c * FF_CHUNK, (c + 1) * FF_CHUNK
        a = _dot(h, win_ref[:, lo:hi].astype(BF16))
        if c < len(side_jobs):
            side_jobs[c]()
        a = jnp.square(jnp.maximum(a, 0.0)).astype(BF16)
        acc = acc + _dot(a, wout_ref[lo:hi, :].astype(BF16))
    return acc


def _oproj_mlp_kernel(x_ref, o_ref, wo_ref, g_ref, win_ref, wout_ref, out_ref):
    x = x_ref[...] + _dot(o_ref[...], wo_ref[...].astype(BF16))
    out_ref[...] = _mlp(x, g_ref, win_ref, wout_ref)


def _resident(shape):
    zeros = (0,) * len(shape)
    return pl.BlockSpec(shape, lambda *_: zeros, pipeline_mode=pl.Buffered(1))


def _resident_layer(stacked, layer):
    index = (layer,) + (0,) * (stacked.ndim - 1)
    return pl.BlockSpec((None,) + stacked.shape[1:], lambda *_: index,
                        pipeline_mode=pl.Buffered(1))


def _oproj_mlp_layer(x2, o2, w_o, attn_layer, g, w_in, w_out, layer):
    n, d = x2.shape
    tm = TOKEN_TILE
    row = pl.BlockSpec((tm, d), lambda i: (i, 0))
    return pl.pallas_call(
        _oproj_mlp_kernel,
        out_shape=jax.ShapeDtypeStruct(x2.shape, x2.dtype),
        grid=(n // tm,),
        in_specs=[row, row, _resident_layer(w_o, attn_layer), _resident(g.shape),
                  _resident_layer(w_in, layer), _resident_layer(w_out, layer)],
        out_specs=row,
        compiler_params=pltpu.CompilerParams(
            dimension_semantics=("arbitrary",), vmem_limit_bytes=VMEM_LIMIT_BYTES),
        name="oproj_mlp",
    )(x2, o2, w_o, g, w_in, w_out)


def _proj_kernel(*refs, with_kv):
    if with_kv:
        (x_ref, gq_ref, wq_ref, avg_ref, qg_ref, gkv_ref, wkv_ref, kn_ref,
         q_ref, kt_ref, v_ref) = refs
    else:
        x_ref, gq_ref, wq_ref, avg_ref, qg_ref, q_ref = refs
    x = x_ref[0]
    xn = x * lax.rsqrt(jnp.mean(x * x, axis=-1, keepdims=True) + EPS)

    q = _dot((xn * gq_ref[...]).astype(BF16), wq_ref[...].astype(BF16))
    sq = (q * q).astype(BF16)
    ms = jnp.concatenate([_dot(sq[:, k:k + MXU_TILE], avg_ref[...])
                          for k in range(0, QK_WIDTH, MXU_TILE)], axis=1)
    q_ref[0] = (q * lax.rsqrt(ms + EPS) * qg_ref[...]).astype(BF16)

    if with_kv:
        h = (xn * gkv_ref[...]).astype(BF16)
        w_k = wkv_ref[:, :QK_WIDTH].astype(BF16)
        kt = lax.dot_general(w_k, h, (((0,), (1,)), ((), ())),
                             preferred_element_type=F32)
        t = kt.shape[-1]
        kt = kt.reshape(QK_WIDTH // HEAD_DIM, HEAD_DIM, t)
        kms = jnp.mean(kt * kt, axis=1, keepdims=True)
        kt = kt * lax.rsqrt(kms + EPS) * kn_ref[...][None]
        kt_ref[0] = kt.reshape(QK_WIDTH, t).astype(BF16)
        v_ref[0] = _dot(h, wkv_ref[:, QK_WIDTH:].astype(BF16)).astype(BF16)


def _projections(x, g_q, w_q, layer, avg, q_gain, kv=None):
    b, s, d = x.shape
    tm = TOKEN_TILE
    tile = lambda width: pl.BlockSpec((1, tm, width), lambda i, j: (i, j, 0))
    args = [x, g_q, w_q, avg, q_gain]
    specs = [tile(d), _resident(g_q.shape), _resident_layer(w_q, layer), _resident(avg.shape),
             _resident(q_gain.shape)]
    out_shape = [jax.ShapeDtypeStruct((b, s, QK_WIDTH), BF16)]
    out_specs = [tile(QK_WIDTH)]
    if kv is not None:
        g_kv, w_kv, k_norm_col = kv
        v_width = w_kv.shape[1] - QK_WIDTH
        args += [g_kv, w_kv, k_norm_col]
        specs += [_resident(g_kv.shape), _resident(w_kv.shape), _resident(k_norm_col.shape)]
        out_shape += [jax.ShapeDtypeStruct((b, QK_WIDTH, s), BF16),
                      jax.ShapeDtypeStruct((b, s, v_width), BF16)]
        out_specs += [pl.BlockSpec((1, QK_WIDTH, tm), lambda i, j: (i, 0, j)), tile(v_width)]
    outs = pl.pallas_call(
        functools.partial(_proj_kernel, with_kv=kv is not None),
        out_shape=tuple(out_shape),
        grid=(b, s // tm),
        in_specs=specs,
        out_specs=tuple(out_specs),
        compiler_params=pltpu.CompilerParams(
            dimension_semantics=("arbitrary", "arbitrary"), vmem_limit_bytes=VMEM_LIMIT_BYTES),
        name="qkv_proj" if kv is not None else "q_proj",
    )(*args)
    return outs if kv is not None else outs[0]


def _build_bias_tiles(rb_ref, bias_ref, t):
    row = lax.broadcasted_iota(jnp.int32, (t, t), 0)
    col = lax.broadcasted_iota(jnp.int32, (t, t), 1)
    for hh in range(N_HEADS):
        last = rb_ref[N_BUCKETS - 1, hh]
        for kind, offset in enumerate((0, t)):
            rel = row - col + offset
            tile = jnp.full((t, t), rb_ref[0, hh], F32)
            for bkt in range(1, N_BUCKETS):
                tile = jnp.where(rel >= BUCKET_START[bkt], rb_ref[bkt, hh], tile)
            tile = (tile - last) * LOG2E
            if offset == 0:
                tile = jnp.where(rel >= 0, tile, MASK_VALUE)
            bias_ref[hh, kind] = tile


def _attn_kernel(rb_ref, q_ref, kt_ref, v_ref, lam_ref, sub_ref, o_ref, bias_ref, vaug_ref,
                 *, lambda_init):
    t = ATTN_TILE
    seq = q_ref.shape[1]
    nq = seq // t
    qk, vd = 2 * HEAD_DIM, V_DIM
    head0 = pl.program_id(1) * HEADS_PER_STEP

    @pl.when((pl.program_id(0) == 0) & (pl.program_id(1) == 0))
    def _():
        _build_bias_tiles(rb_ref, bias_ref, t)

    lane_v = lax.broadcasted_iota(jnp.int32, (seq, vd), 1)
    for a in range(HEADS_PER_STEP):
        vaug_ref[a, :, :vd] = v_ref[0, :, a * vd:(a + 1) * vd]
        vaug_ref[a, :, vd:] = jnp.where(lane_v == 0, 1.0, 0.0).astype(BF16)

    lam_p = lam_ref[...]
    lam = (jnp.exp(jnp.sum(lam_p[0:1] * lam_p[1:2], axis=-1, keepdims=True))
           - jnp.exp(jnp.sum(lam_p[2:3] * lam_p[3:4], axis=-1, keepdims=True))
           + lambda_init)
    lane_q = lax.broadcasted_iota(jnp.int32, (t, 2 * HEAD_DIM), 1)

    def scores(a, i):
        q = q_ref[0, i * t:(i + 1) * t, a * qk:(a + 1) * qk]
        zero = jnp.zeros_like(q)
        q_stack = jnp.concatenate([jnp.where(lane_q < HEAD_DIM, q, zero),
                                   jnp.where(lane_q >= HEAD_DIM, q, zero)], axis=0)
        return _dot(q_stack, kt_ref[0, a * qk:(a + 1) * qk, :(i + 1) * t])

    def probabilities(a, i, s):
        hh = head0 + a
        ps = []
        for c in range(2):
            cols = [s[c * t:(c + 1) * t, kb * t:(kb + 1) * t] for kb in range(i + 1)]
            cols[i] = cols[i] + bias_ref[hh, 0]
            if i >= 1:
                cols[i - 1] = cols[i - 1] + bias_ref[hh, 1]
            sc = jnp.concatenate(cols, axis=1) if i >= 1 else cols[0]
            m = jnp.max(sc, axis=-1, keepdims=True)
            ps.append(jnp.exp2(sc - m).astype(BF16))
        return jnp.concatenate(ps, axis=0)

    def output(a, i, p):
        acc = _dot(p, vaug_ref[a, :(i + 1) * t, :])
        outs = [acc[c * t:(c + 1) * t, :vd] / acc[c * t:(c + 1) * t, vd:vd + 1]
                for c in range(2)]
        o = outs[0] - lam * outs[1]
        o = _rmsnorm(o, sub_ref[...]) * (1.0 - lambda_init)
        o_ref[0, i * t:(i + 1) * t, a * vd:(a + 1) * vd] = o.astype(BF16)

    order = [1] + list(range(nq - 1, 1, -1)) + [0]
    assert sorted(order) == list(range(nq))
    blocks = [(a, i) for i in order for a in range(HEADS_PER_STEP)]
    s_cur = p_cur = None
    for n in range(len(blocks) + 2):
        p_next = probabilities(*blocks[n - 1], s_cur) if 1 <= n <= len(blocks) else None
        s_next = scores(*blocks[n]) if n < len(blocks) else None
        if n >= 2:
            output(*blocks[n - 2], p_cur)
        s_cur, p_cur = s_next, p_next


def _diff_attention(rel_bias, q, kt, v, lam_params, subln, lambda_init):
    b, s, _ = q.shape
    t = ATTN_TILE
    hps = HEADS_PER_STEP
    assert t > FAR_DISTANCE and s % t == 0 and N_HEADS % hps == 0
    return pl.pallas_call(
        functools.partial(_attn_kernel, lambda_init=lambda_init),
        out_shape=jax.ShapeDtypeStruct((b, s, N_HEADS * V_DIM), BF16),
        grid=(b, N_HEADS // hps),
        in_specs=[
            pl.BlockSpec(memory_space=pltpu.SMEM),
            pl.BlockSpec((1, s, hps * 2 * HEAD_DIM), lambda i, j: (i, 0, j)),
            pl.BlockSpec((1, hps * 2 * HEAD_DIM, s), lambda i, j: (i, j, 0)),
            pl.BlockSpec((1, s, hps * V_DIM), lambda i, j: (i, 0, j)),
            pl.BlockSpec(lam_params.shape, lambda i, j: (0, 0)),
            pl.BlockSpec(subln.shape, lambda i, j: (0, 0)),
        ],
        out_specs=pl.BlockSpec((1, s, hps * V_DIM), lambda i, j: (i, 0, j)),
        scratch_shapes=[pltpu.VMEM((N_HEADS, 2, t, t), F32),
                        pltpu.VMEM((hps, s, 2 * V_DIM), BF16)],
        compiler_params=pltpu.CompilerParams(
            dimension_semantics=("arbitrary", "arbitrary"), vmem_limit_bytes=VMEM_LIMIT_BYTES),
        name="diff_attention",
    )(rel_bias, q, kt, v, lam_params, subln)


def kernel(x, norm_mix, norm_mlp, pool_w, pool_scale, kv_norm, w_kv, k_norm, rel_bias,
           w_q, q_norm, lam_q1, lam_k1, lam_q2, lam_k2, subln, w_o, w_mlp_in, w_mlp_out):
    b, s, d = x.shape
    n = b * s
    row = lambda a: a.reshape(1, -1).astype(F32)

    seg = np.arange(MXU_TILE) // HEAD_DIM
    avg = jnp.asarray((seg[:, None] == seg[None, :]).astype(np.float32) / HEAD_DIM, dtype=BF16)

    assert s % TOKEN_TILE == 0 and TOKEN_TILE >= MAX_WINDOW
    x2 = x.reshape(n, d)
    for l in range(N_A):
        x2 = _pool_mlp_layer(x2, row(norm_mix[l]), pool_w, row(pool_scale[l]), row(norm_mlp[l]),
                             w_mlp_in, w_mlp_out, l, s // TOKEN_TILE)

    kt = v = None
    for l in range(N_A, DEPTH):
        j = l - N_A
        lambda_init = 0.8 - 0.6 * math.exp(-0.3 * l)
        q_gain = row(jnp.tile(q_norm[j], QK_WIDTH // HEAD_DIM)) * (HEAD_DIM ** -0.5 * LOG2E)
        x3 = x2.reshape(b, s, d)
        if l == N_A:
            q, kt, v = _projections(x3, row(norm_mix[l]), w_q, j, avg, q_gain,
                                    kv=(row(kv_norm), w_kv, k_norm.reshape(HEAD_DIM, 1).astype(F32)))
        else:
            q = _projections(x3, row(norm_mix[l]), w_q, j, avg, q_gain)
        lam_params = jnp.stack([lam_q1[j], lam_k1[j], lam_q2[j], lam_k2[j]]).astype(F32)
        o = _diff_attention(rel_bias.astype(F32), q, kt, v, lam_params, row(subln[j]), lambda_init)
        x2 = _oproj_mlp_layer(x2, o.reshape(n, N_HEADS * V_DIM), w_o, j, row(norm_mlp[l]),
                              w_mlp_in, w_mlp_out, l)
    return x2.reshape(b, s, d)
```

```python
import functools
import math

import numpy as np
import jax
import jax.numpy as jnp
from jax import lax
from jax.experimental import pallas as pl
from jax.experimental.pallas import tpu as pltpu

D_MODEL = 1024
DEPTH = 4
N_A = DEPTH // 2
POOL_WINDOWS = (2, 4, 8, 16)
N_GROUPS = len(POOL_WINDOWS)
GROUP_DIM = D_MODEL // N_GROUPS
N_HEADS = 8
HEAD_DIM = D_MODEL // (2 * N_HEADS)
V_DIM = 2 * HEAD_DIM
QK_WIDTH = N_HEADS * 2 * HEAD_DIM
D_FF = 4 * D_MODEL
N_BUCKETS = 32
MAX_DISTANCE = 128
EPS = 1e-6

LOG2E = math.log2(math.e)
MASK_VALUE = -1e30
MAX_WINDOW = max(POOL_WINDOWS)
LANES = 128
MXU_TILE = 256

VMEM_LIMIT_BYTES = 56 * 1024 * 1024

TOKEN_TILE = 512
FF_CHUNK = 1024
PROJ_TOKEN_TILE = 1024
ATTN_TILE = 256
HEADS_PER_STEP = 2

F32 = jnp.float32
BF16 = jnp.bfloat16


def _bucket_thresholds():
    max_exact = N_BUCKETS // 2
    n = np.arange(1, 4 * MAX_DISTANCE, dtype=np.float64)
    v = np.log(n / max_exact) / math.log(MAX_DISTANCE / max_exact) * (N_BUCKETS - max_exact)
    frac = v - np.floor(v)
    margin = np.minimum(frac, 1.0 - frac)[max_exact:MAX_DISTANCE - 1]
    assert margin.min() > 1e-3
    large = np.minimum(max_exact + v.astype(np.int64), N_BUCKETS - 1)
    bucket = np.where(n < max_exact, n.astype(np.int64), large)
    thr = [0] + [int(np.argmax(bucket >= b)) + 1 for b in range(1, N_BUCKETS)]
    return tuple(thr)


BUCKET_START = _bucket_thresholds()
FAR_DISTANCE = BUCKET_START[-1]


def _rmsnorm(xf, g):
    ms = jnp.mean(xf * xf, axis=-1, keepdims=True)
    return xf * lax.rsqrt(ms + EPS) * g


def _dot(a, b):
    return jnp.dot(a, b, preferred_element_type=F32)


def _pool_prepare(x_ref, seq_tile_idx, g_ref, carry_ref, ext_ref):
    t = x_ref.shape[0]
    h = _rmsnorm(x_ref[...], g_ref[...])
    ext_ref[:MAX_WINDOW, :] = jnp.where(seq_tile_idx == 0, 0.0, carry_ref[...])
    ext_ref[MAX_WINDOW:, :] = h
    carry_ref[...] = h[t - MAX_WINDOW:, :]


def _pool_group(g, x_ref, seq_tile_idx, pw_ref, ps_ref, ext_ref, out_ref):
    t = x_ref.shape[0]
    w = POOL_WINDOWS[g]
    lo, hi = g * GROUP_DIM, (g + 1) * GROUP_DIM
    s = ext_ref[:, lo:hi]
    h = s[MAX_WINDOW:, :]
    span = 1
    while span < w:
        s = s + pltpu.roll(s, span, axis=0)
        span *= 2
    pos = seq_tile_idx * t + lax.broadcasted_iota(jnp.int32, (t, 1), 0)
    cnt = jnp.minimum(pos + 1, w).astype(F32)
    m = s[MAX_WINDOW:, :] / cnt - h
    y = _dot(m.astype(BF16), pw_ref[g].astype(BF16))
    out_ref[:, lo:hi] = x_ref[:, lo:hi] + y * ps_ref[:, lo:hi]


def _pool_mlp_kernel(x0_ref, xn_ref, gmix_ref, pw_ref, ps_ref, g_ref, win_ref, wout_ref, out_ref,
                     carry_ref, ext_ref, mixed_ref, *, tiles_per_seq):
    k = pl.program_id(0)
    last = pl.num_programs(0) - 1

    @pl.when(k == 0)
    def _():
        _pool_prepare(x0_ref, 0, gmix_ref, carry_ref, ext_ref)
        for g in range(N_GROUPS):
            _pool_group(g, x0_ref, 0, pw_ref, ps_ref, ext_ref, mixed_ref.at[0])

    nxt = jnp.minimum(k + 1, last) % tiles_per_seq
    _pool_prepare(xn_ref, nxt, gmix_ref, carry_ref, ext_ref)
    side_jobs = [functools.partial(_pool_group, g, xn_ref, nxt, pw_ref, ps_ref, ext_ref,
                                   mixed_ref.at[(k + 1) % 2]) for g in range(N_GROUPS)]
    out_ref[...] = _mlp(mixed_ref[k % 2], g_ref, win_ref, wout_ref, side_jobs)


def _pool_mlp_layer(x2, g_mix, pw, ps, g, w_in, w_out, layer, tiles_per_seq):
    n, d = x2.shape
    tm = TOKEN_TILE
    last = n // tm - 1
    return pl.pallas_call(
        functools.partial(_pool_mlp_kernel, tiles_per_seq=tiles_per_seq),
        out_shape=jax.ShapeDtypeStruct(x2.shape, x2.dtype),
        grid=(n // tm,),
        in_specs=[
            pl.BlockSpec((tm, d), lambda i: (0, 0), pipeline_mode=pl.Buffered(1)),
            pl.BlockSpec((tm, d), lambda i: (jnp.minimum(i + 1, last), 0)),
            _resident(g_mix.shape), _resident_layer(pw, layer), _resident(ps.shape),
            _resident(g.shape), _resident_layer(w_in, layer), _resident_layer(w_out, layer),
        ],
        out_specs=pl.BlockSpec((tm, d), lambda i: (i, 0)),
        scratch_shapes=[pltpu.VMEM((MAX_WINDOW, d), F32),
                        pltpu.VMEM((MAX_WINDOW + tm, d), F32),
                        pltpu.VMEM((2, tm, d), F32)],
        compiler_params=pltpu.CompilerParams(
            dimension_semantics=("arbitrary",), vmem_limit_bytes=VMEM_LIMIT_BYTES),
        name="pool_mlp",
    )(x2, x2, g_mix, pw, ps, g, w_in, w_out)


def _mlp(x, g_ref, win_ref, wout_ref, side_jobs=()):
    h = _rmsnorm(x, g_ref[...]).astype(BF16)
    acc = x
    for c in range(D_FF // FF_CHUNK):
        lo, hi = c * FF_CHUNK, (c + 1) * FF_CHUNK
        a = _dot(h, win_ref[:, lo:hi].astype(BF16))
        if c < len(side_jobs):
            side_jobs[c]()
        a = jnp.square(jnp.maximum(a, 0.0)).astype(BF16)
        acc = acc + _dot(a, wout_ref[lo:hi, :].astype(BF16))
    return acc


def _oproj_mlp_kernel(x_ref, o_ref, wo_ref, g_ref, win_ref, wout_ref, out_ref):
    x = x_ref[...] + _dot(o_ref[...], wo_ref[...].astype(BF16))
    out_ref[...] = _mlp(x, g_ref, win_ref, wout_ref)


def _resident(shape):
    zeros = (0,) * len(shape)
    return pl.BlockSpec(shape, lambda *_: zeros, pipeline_mode=pl.Buffered(1))


def _resident_layer(stacked, layer):
    index = (layer,) + (0,) * (stacked.ndim - 1)
    return pl.BlockSpec((None,) + stacked.shape[1:], lambda *_: index,
                        pipeline_mode=pl.Buffered(1))


def _oproj_mlp_layer(x2, o2, w_o, attn_layer, g, w_in, w_out, layer):
    n, d = x2.shape
    tm = TOKEN_TILE
    row = pl.BlockSpec((tm, d), lambda i: (i, 0))
    return pl.pallas_call(
        _oproj_mlp_kernel,
        out_shape=jax.ShapeDtypeStruct(x2.shape, x2.dtype),
        grid=(n // tm,),
        in_specs=[row, row, _resident_layer(w_o, attn_layer), _resident(g.shape),
                  _resident_layer(w_in, layer), _resident_layer(w_out, layer)],
        out_specs=row,
        compiler_params=pltpu.CompilerParams(
            dimension_semantics=("arbitrary",), vmem_limit_bytes=VMEM_LIMIT_BYTES),
        name="oproj_mlp",
    )(x2, o2, w_o, g, w_in, w_out)


def _proj_kernel(*refs, with_kv):
    if with_kv:
        (x_ref, gq_ref, wq_ref, avg_ref, qg_ref, gkv_ref, wkv_ref, kn_ref,
         q_ref, kt_ref, v_ref) = refs
    else:
        x_ref, gq_ref, wq_ref, avg_ref, qg_ref, q_ref = refs
    x = x_ref[0]
    xn = x * lax.rsqrt(jnp.mean(x * x, axis=-1, keepdims=True) + EPS)

    q = _dot((xn * gq_ref[...]).astype(BF16), wq_ref[...].astype(BF16))
    sq = (q * q).astype(BF16)
    ms = jnp.concatenate([_dot(sq[:, k:k + MXU_TILE], avg_ref[...])
                          for k in range(0, QK_WIDTH, MXU_TILE)], axis=1)
    q_ref[0] = (q * lax.rsqrt(ms + EPS) * qg_ref[...]).astype(BF16)

    if with_kv:
        h = (xn * gkv_ref[...]).astype(BF16)
        w_k = wkv_ref[:, :QK_WIDTH].astype(BF16)
        kt = lax.dot_general(w_k, h, (((0,), (1,)), ((), ())),
                             preferred_element_type=F32)
        t = kt.shape[-1]
        kt = kt.reshape(QK_WIDTH // HEAD_DIM, HEAD_DIM, t)
        kms = jnp.mean(kt * kt, axis=1, keepdims=True)
        kt = kt * lax.rsqrt(kms + EPS) * kn_ref[...][None]
        kt_ref[0] = kt.reshape(QK_WIDTH, t).astype(BF16)
        v_ref[0] = _dot(h, wkv_ref[:, QK_WIDTH:].astype(BF16)).astype(BF16)


def _projections(x, g_q, w_q, layer, avg, q_gain, kv=None):
    b, s, d = x.shape
    tm = PROJ_TOKEN_TILE
    tile = lambda width: pl.BlockSpec((1, tm, width), lambda i, j: (i, j, 0))
    args = [x, g_q, w_q, avg, q_gain]
    specs = [tile(d), _resident(g_q.shape), _resident_layer(w_q, layer), _resident(avg.shape),
             _resident(q_gain.shape)]
    out_shape = [jax.ShapeDtypeStruct((b, s, QK_WIDTH), BF16)]
    out_specs = [tile(QK_WIDTH)]
    if kv is not None:
        g_kv, w_kv, k_norm_col = kv
        v_width = w_kv.shape[1] - QK_WIDTH
        args += [g_kv, w_kv, k_norm_col]
        specs += [_resident(g_kv.shape), _resident(w_kv.shape), _resident(k_norm_col.shape)]
        out_shape += [jax.ShapeDtypeStruct((b, QK_WIDTH, s), BF16),
                      jax.ShapeDtypeStruct((b, s, v_width), BF16)]
        out_specs += [pl.BlockSpec((1, QK_WIDTH, tm), lambda i, j: (i, 0, j)), tile(v_width)]
    outs = pl.pallas_call(
        functools.partial(_proj_kernel, with_kv=kv is not None),
        out_shape=tuple(out_shape),
        grid=(b, s // tm),
        in_specs=specs,
        out_specs=tuple(out_specs),
        compiler_params=pltpu.CompilerParams(
            dimension_semantics=("arbitrary", "arbitrary"), vmem_limit_bytes=VMEM_LIMIT_BYTES),
        name="qkv_proj" if kv is not None else "q_proj",
    )(*args)
    return outs if kv is not None else outs[0]


def _build_bias_tiles(rb_ref, bias_ref, t):
    row = lax.broadcasted_iota(jnp.int32, (t, t), 0)
    col = lax.broadcasted_iota(jnp.int32, (t, t), 1)
    for hh in range(N_HEADS):
        last = rb_ref[N_BUCKETS - 1, hh]
        for kind, offset in enumerate((0, t)):
            rel = row - col + offset
            tile = jnp.full((t, t), rb_ref[0, hh], F32)
            for bkt in range(1, N_BUCKETS):
                tile = jnp.where(rel >= BUCKET_START[bkt], rb_ref[bkt, hh], tile)
            tile = (tile - last) * LOG2E
            if offset == 0:
                tile = jnp.where(rel >= 0, tile, MASK_VALUE)
            bias_ref[hh, kind] = tile


def _attn_kernel(rb_ref, q_ref, kt_ref, v_ref, lam_ref, sub_ref, o_ref, bias_ref, vaug_ref,
                 *, lambda_init):
    t = ATTN_TILE
    seq = q_ref.shape[1]
    nq = seq // t
    qk, vd = 2 * HEAD_DIM, V_DIM
    head0 = pl.program_id(1) * HEADS_PER_STEP

    @pl.when((pl.program_id(0) == 0) & (pl.program_id(1) == 0))
    def _():
        _build_bias_tiles(rb_ref, bias_ref, t)

    lane_v = lax.broadcasted_iota(jnp.int32, (seq, vd), 1)
    for a in range(HEADS_PER_STEP):
        vaug_ref[a, :, :vd] = v_ref[0, :, a * vd:(a + 1) * vd]
        vaug_ref[a, :, vd:] = jnp.where(lane_v == 0, 1.0, 0.0).astype(BF16)

    lam_p = lam_ref[...]
    lam = (jnp.exp(jnp.sum(lam_p[0:1] * lam_p[1:2], axis=-1, keepdims=True))
           - jnp.exp(jnp.sum(lam_p[2:3] * lam_p[3:4], axis=-1, keepdims=True))
           + lambda_init)
    lane_q = lax.broadcasted_iota(jnp.int32, (t, 2 * HEAD_DIM), 1)

    def scores(a, i):
        q = q_ref[0, i * t:(i + 1) * t, a * qk:(a + 1) * qk]
        zero = jnp.zeros_like(q)
        q_stack = jnp.concatenate([jnp.where(lane_q < HEAD_DIM, q, zero),
                                   jnp.where(lane_q >= HEAD_DIM, q, zero)], axis=0)
        return _dot(q_stack, kt_ref[0, a * qk:(a + 1) * qk, :(i + 1) * t])

    def probabilities(a, i, s):
        hh = head0 + a
        ps = []
        for c in range(2):
            cols = [s[c * t:(c + 1) * t, kb * t:(kb + 1) * t] for kb in range(i + 1)]
            cols[i] = cols[i] + bias_ref[hh, 0]
            if i >= 1:
                cols[i - 1] = cols[i - 1] + bias_ref[hh, 1]
            sc = jnp.concatenate(cols, axis=1) if i >= 1 else cols[0]
            m = jnp.max(sc, axis=-1, keepdims=True)
            ps.append(jnp.exp2(sc - m).astype(BF16))
        return jnp.concatenate(ps, axis=0)

    def output(a, i, p):
        acc = _dot(p, vaug_ref[a, :(i + 1) * t, :])
        outs = [acc[c * t:(c + 1) * t, :vd] / acc[c * t:(c + 1) * t, vd:vd + 1]
                for c in range(2)]
        o = outs[0] - lam * outs[1]
        o = _rmsnorm(o, sub_ref[...]) * (1.0 - lambda_init)
        o_ref[0, i * t:(i + 1) * t, a * vd:(a + 1) * vd] = o.astype(BF16)

    order = [1] + list(range(nq - 1, 1, -1)) + [0]
    assert sorted(order) == list(range(nq))
    blocks = [(a, i) for i in order for a in range(HEADS_PER_STEP)]
    s_cur = p_cur = None
    for n in range(len(blocks) + 2):
        p_next = probabilities(*blocks[n - 1], s_cur) if 1 <= n <= len(blocks) else None
        s_next = scores(*blocks[n]) if n < len(blocks) else None
        if n >= 2:
            output(*blocks[n - 2], p_cur)
        s_cur, p_cur = s_next, p_next


def _diff_attention(rel_bias, q, kt, v, lam_params, subln, lambda_init):
    b, s, _ = q.shape
    t = ATTN_TILE
    hps = HEADS_PER_STEP
    assert t > FAR_DISTANCE and s % t == 0 and N_HEADS % hps == 0
    return pl.pallas_call(
        functools.partial(_attn_kernel, lambda_init=lambda_init),
        out_shape=jax.ShapeDtypeStruct((b, s, N_HEADS * V_DIM), BF16),
        grid=(b, N_HEADS // hps),
        in_specs=[
            pl.BlockSpec(memory_space=pltpu.SMEM),
            pl.BlockSpec((1, s, hps * 2 * HEAD_DIM), lambda i, j: (i, 0, j)),
            pl.BlockSpec((1, hps * 2 * HEAD_DIM, s), lambda i, j: (i, j, 0)),
            pl.BlockSpec((1, s, hps * V_DIM), lambda i, j: (i, 0, j)),
            pl.BlockSpec(lam_params.shape, lambda i, j: (0, 0)),
            pl.BlockSpec(subln.shape, lambda i, j: (0, 0)),
        ],
        out_specs=pl.BlockSpec((1, s, hps * V_DIM), lambda i, j: (i, 0, j)),
        scratch_shapes=[pltpu.VMEM((N_HEADS, 2, t, t), F32),
                        pltpu.VMEM((hps, s, 2 * V_DIM), BF16)],
        compiler_params=pltpu.CompilerParams(
            dimension_semantics=("arbitrary", "arbitrary"), vmem_limit_bytes=VMEM_LIMIT_BYTES),
        name="diff_attention",
    )(rel_bias, q, kt, v, lam_params, subln)


def kernel(x, norm_mix, norm_mlp, pool_w, pool_scale, kv_norm, w_kv, k_norm, rel_bias,
           w_q, q_norm, lam_q1, lam_k1, lam_q2, lam_k2, subln, w_o, w_mlp_in, w_mlp_out):
    b, s, d = x.shape
    n = b * s
    row = lambda a: a.reshape(1, -1).astype(F32)

    seg = np.arange(MXU_TILE) // HEAD_DIM
    avg = jnp.asarray((seg[:, None] == seg[None, :]).astype(np.float32) / HEAD_DIM, dtype=BF16)

    assert s % TOKEN_TILE == 0 and TOKEN_TILE >= MAX_WINDOW
    x2 = x.reshape(n, d)
    for l in range(N_A):
        x2 = _pool_mlp_layer(x2, row(norm_mix[l]), pool_w, row(pool_scale[l]), row(norm_mlp[l]),
                             w_mlp_in, w_mlp_out, l, s // TOKEN_TILE)

    kt = v = None
    for l in range(N_A, DEPTH):
        j = l - N_A
        lambda_init = 0.8 - 0.6 * math.exp(-0.3 * l)
        q_gain = row(jnp.tile(q_norm[j], QK_WIDTH // HEAD_DIM)) * (HEAD_DIM ** -0.5 * LOG2E)
        x3 = x2.reshape(b, s, d)
        if l == N_A:
            q, kt, v = _projections(x3, row(norm_mix[l]), w_q, j, avg, q_gain,
                                    kv=(row(kv_norm), w_kv, k_norm.reshape(HEAD_DIM, 1).astype(F32)))
        else:
            q = _projections(x3, row(norm_mix[l]), w_q, j, avg, q_gain)
        lam_params = jnp.stack([lam_q1[j], lam_k1[j], lam_q2[j], lam_k2[j]]).astype(F32)
        o = _diff_attention(rel_bias.astype(F32), q, kt, v, lam_params, row(subln[j]), lambda_init)
        x2 = _oproj_mlp_layer(x2, o.reshape(n, N_HEADS * V_DIM), w_o, j, row(norm_mlp[l]),
                              w_mlp_in, w_mlp_out, l)
    return x2.reshape(b, s, d)
```

```python
import functools
import math

import numpy as np
import jax
import jax.numpy as jnp
from jax import lax
from jax.experimental import pallas as pl
from jax.experimental.pallas import tpu as pltpu

D_MODEL = 1024
DEPTH = 4
N_A = DEPTH // 2
POOL_WINDOWS = (2, 4, 8, 16)
N_GROUPS = len(POOL_WINDOWS)
GROUP_DIM = D_MODEL // N_GROUPS
N_HEADS = 8
HEAD_DIM = D_MODEL // (2 * N_HEADS)
V_DIM = 2 * HEAD_DIM
QK_WIDTH = N_HEADS * 2 * HEAD_DIM
D_FF = 4 * D_MODEL
N_BUCKETS = 32
MAX_DISTANCE = 128
EPS = 1e-6

LOG2E = math.log2(math.e)
MASK_VALUE = -1e30
MAX_WINDOW = max(POOL_WINDOWS)
LANES = 128
MXU_TILE = 256

VMEM_LIMIT_BYTES = 56 * 1024 * 1024

TOKEN_TILE = 512
FF_CHUNK = 1024
PROJ_TOKEN_TILE = 1024
OPROJ_TOKEN_TILE = 1024
ATTN_TILE = 256
HEADS_PER_STEP = 2

F32 = jnp.float32
BF16 = jnp.bfloat16


def _bucket_thresholds():
    max_exact = N_BUCKETS // 2
    n = np.arange(1, 4 * MAX_DISTANCE, dtype=np.float64)
    v = np.log(n / max_exact) / math.log(MAX_DISTANCE / max_exact) * (N_BUCKETS - max_exact)
    frac = v - np.floor(v)
    margin = np.minimum(frac, 1.0 - frac)[max_exact:MAX_DISTANCE - 1]
    assert margin.min() > 1e-3
    large = np.minimum(max_exact + v.astype(np.int64), N_BUCKETS - 1)
    bucket = np.where(n < max_exact, n.astype(np.int64), large)
    thr = [0] + [int(np.argmax(bucket >= b)) + 1 for b in range(1, N_BUCKETS)]
    return tuple(thr)


BUCKET_START = _bucket_thresholds()
FAR_DISTANCE = BUCKET_START[-1]


def _rmsnorm(xf, g):
    ms = jnp.mean(xf * xf, axis=-1, keepdims=True)
    return xf * lax.rsqrt(ms + EPS) * g


def _dot(a, b):
    return jnp.dot(a, b, preferred_element_type=F32)


def _pool_prepare(x_ref, seq_tile_idx, g_ref, carry_ref, ext_ref):
    t = x_ref.shape[0]
    h = _rmsnorm(x_ref[...], g_ref[...])
    ext_ref[:MAX_WINDOW, :] = jnp.where(seq_tile_idx == 0, 0.0, carry_ref[...])
    ext_ref[MAX_WINDOW:, :] = h
    carry_ref[...] = h[t - MAX_WINDOW:, :]


def _pool_group(g, x_ref, seq_tile_idx, pw_ref, ps_ref, ext_ref, out_ref):
    t = x_ref.shape[0]
    w = POOL_WINDOWS[g]
    lo, hi = g * GROUP_DIM, (g + 1) * GROUP_DIM
    s = ext_ref[:, lo:hi]
    h = s[MAX_WINDOW:, :]
    span = 1
    while span < w:
        s = s + pltpu.roll(s, span, axis=0)
        span *= 2
    pos = seq_tile_idx * t + lax.broadcasted_iota(jnp.int32, (t, 1), 0)
    cnt = jnp.minimum(pos + 1, w).astype(F32)
    m = s[MAX_WINDOW:, :] / cnt - h
    y = _dot(m.astype(BF16), pw_ref[g].astype(BF16))
    out_ref[:, lo:hi] = x_ref[:, lo:hi] + y * ps_ref[:, lo:hi]


def _pool_mlp_kernel(x0_ref, xn_ref, gmix_ref, pw_ref, ps_ref, g_ref, win_ref, wout_ref, out_ref,
                     carry_ref, ext_ref, mixed_ref, *, tiles_per_seq):
    k = pl.program_id(0)
    last = pl.num_programs(0) - 1

    @pl.when(k == 0)
    def _():
        _pool_prepare(x0_ref, 0, gmix_ref, carry_ref, ext_ref)
        for g in range(N_GROUPS):
            _pool_group(g, x0_ref, 0, pw_ref, ps_ref, ext_ref, mixed_ref.at[0])

    nxt = jnp.minimum(k + 1, last) % tiles_per_seq
    _pool_prepare(xn_ref, nxt, gmix_ref, carry_ref, ext_ref)
    side_jobs = [functools.partial(_pool_group, g, xn_ref, nxt, pw_ref, ps_ref, ext_ref,
                                   mixed_ref.at[(k + 1) % 2]) for g in range(N_GROUPS)]
    out_ref[...] = _mlp(mixed_ref[k % 2], g_ref, win_ref, wout_ref, side_jobs)


def _pool_mlp_layer(x2, g_mix, pw, ps, g, w_in, w_out, layer, tiles_per_seq):
    n, d = x2.shape
    tm = TOKEN_TILE
    last = n // tm - 1
    return pl.pallas_call(
        functools.partial(_pool_mlp_kernel, tiles_per_seq=tiles_per_seq),
        out_shape=jax.ShapeDtypeStruct(x2.shape, x2.dtype),
        grid=(n // tm,),
        in_specs=[
            pl.BlockSpec((tm, d), lambda i: (0, 0), pipeline_mode=pl.Buffered(1)),
            pl.BlockSpec((tm, d), lambda i: (jnp.minimum(i + 1, last), 0)),
            _resident(g_mix.shape), _resident_layer(pw, layer), _resident(ps.shape),
            _resident(g.shape), _resident_layer(w_in, layer), _resident_layer(w_out, layer),
        ],
        out_specs=pl.BlockSpec((tm, d), lambda i: (i, 0)),
        scratch_shapes=[pltpu.VMEM((MAX_WINDOW, d), F32),
                        pltpu.VMEM((MAX_WINDOW + tm, d), F32),
                        pltpu.VMEM((2, tm, d), F32)],
        compiler_params=pltpu.CompilerParams(
            dimension_semantics=("arbitrary",), vmem_limit_bytes=VMEM_LIMIT_BYTES),
        name="pool_mlp",
    )(x2, x2, g_mix, pw, ps, g, w_in, w_out)


def _mlp(x, g_ref, win_ref, wout_ref, side_jobs=()):
    h = _rmsnorm(x, g_ref[...]).astype(BF16)
    acc = x
    for c in range(D_FF // FF_CHUNK):
        lo, hi = c * FF_CHUNK, (c + 1) * FF_CHUNK
        a = _dot(h, win_ref[:, lo:hi].astype(BF16))
        if c < len(side_jobs):
            side_jobs[c]()
        a = jnp.square(jnp.maximum(a, 0.0)).astype(BF16)
        acc = acc + _dot(a, wout_ref[lo:hi, :].astype(BF16))
    return acc


def _oproj_mlp_kernel(x_ref, o_ref, wo_ref, g_ref, win_ref, wout_ref, out_ref):
    x = x_ref[...] + _dot(o_ref[...], wo_ref[...].astype(BF16))
    out_ref[...] = _mlp(x, g_ref, win_ref, wout_ref)


def _resident(shape):
    zeros = (0,) * len(shape)
    return pl.BlockSpec(shape, lambda *_: zeros, pipeline_mode=pl.Buffered(1))


def _resident_layer(stacked, layer):
    index = (layer,) + (0,) * (stacked.ndim - 1)
    return pl.BlockSpec((None,) + stacked.shape[1:], lambda *_: index,
                        pipeline_mode=pl.Buffered(1))


def _oproj_mlp_layer(x2, o2, w_o, g, w_in, w_out):
    n, d = x2.shape
    tm = OPROJ_TOKEN_TILE
    row = pl.BlockSpec((tm, d), lambda i: (i, 0))
    return pl.pallas_call(
        _oproj_mlp_kernel,
        out_shape=jax.ShapeDtypeStruct(x2.shape, x2.dtype),
        grid=(n // tm,),
        in_specs=[row, row, _resident(w_o.shape), _resident(g.shape),
                  _resident(w_in.shape), _resident(w_out.shape)],
        out_specs=row,
        compiler_params=pltpu.CompilerParams(
            dimension_semantics=("arbitrary",), vmem_limit_bytes=VMEM_LIMIT_BYTES),
        name="oproj_mlp",
    )(x2, o2, w_o, g, w_in, w_out)


def _proj_kernel(*refs, with_kv):
    n_cast = 3
    cast_in, refs = refs[:n_cast], refs[n_cast:]
    if with_kv:
        (x_ref, gq_ref, wq_ref, avg_ref, qg_ref, gkv_ref, wkv_ref, kn_ref,
         q_ref, kt_ref, v_ref, *cast_out) = refs
    else:
        x_ref, gq_ref, wq_ref, avg_ref, qg_ref, q_ref, *cast_out = refs
    for src, dst in zip(cast_in, cast_out):
        dst[...] = src[...].astype(BF16)
    x = x_ref[0]
    xn = x * lax.rsqrt(jnp.mean(x * x, axis=-1, keepdims=True) + EPS)

    q = _dot((xn * gq_ref[...]).astype(BF16), wq_ref[...].astype(BF16))
    sq = (q * q).astype(BF16)
    ms = jnp.concatenate([_dot(sq[:, k:k + MXU_TILE], avg_ref[...])
                          for k in range(0, QK_WIDTH, MXU_TILE)], axis=1)
    q_ref[0] = (q * lax.rsqrt(ms + EPS) * qg_ref[...]).astype(BF16)

    if with_kv:
        h = (xn * gkv_ref[...]).astype(BF16)
        w_k = wkv_ref[:, :QK_WIDTH].astype(BF16)
        kt = lax.dot_general(w_k, h, (((0,), (1,)), ((), ())),
                             preferred_element_type=F32)
        t = kt.shape[-1]
        kt = kt.reshape(QK_WIDTH // HEAD_DIM, HEAD_DIM, t)
        kms = jnp.mean(kt * kt, axis=1, keepdims=True)
        kt = kt * lax.rsqrt(kms + EPS) * kn_ref[...][None]
        kt_ref[0] = kt.reshape(QK_WIDTH, t).astype(BF16)
        v_ref[0] = _dot(h, wkv_ref[:, QK_WIDTH:].astype(BF16)).astype(BF16)


def _projections(x, g_q, w_q, layer, avg, q_gain, cast, kv=None):
    b, s, d = x.shape
    tm = PROJ_TOKEN_TILE
    steps_per_seq = s // tm
    n_steps = b * steps_per_seq
    tile = lambda width: pl.BlockSpec((1, tm, width), lambda i, j: (i, j, 0))
    args, specs, cast_shapes, cast_specs = [], [], [], []
    for w, w_layer in cast:
        rows, cols = w.shape[1] // n_steps, w.shape[2]
        assert w.shape[1] % n_steps == 0 and rows % 16 == 0
        args.append(w)
        specs.append(pl.BlockSpec((None, rows, cols),
                                  lambda i, j, w_layer=w_layer: (w_layer, i * steps_per_seq + j, 0)))
        cast_shapes.append(jax.ShapeDtypeStruct(w.shape[1:], BF16))
        cast_specs.append(pl.BlockSpec((rows, cols), lambda i, j: (i * steps_per_seq + j, 0)))
    args += [x, g_q, w_q, avg, q_gain]
    specs += [tile(d), _resident(g_q.shape), _resident_layer(w_q, layer), _resident(avg.shape),
              _resident(q_gain.shape)]
    out_shape = [jax.ShapeDtypeStruct((b, s, QK_WIDTH), BF16)]
    out_specs = [tile(QK_WIDTH)]
    if kv is not None:
        g_kv, w_kv, k_norm_col = kv
        v_width = w_kv.shape[1] - QK_WIDTH
        args += [g_kv, w_kv, k_norm_col]
        specs += [_resident(g_kv.shape), _resident(w_kv.shape), _resident(k_norm_col.shape)]
        out_shape += [jax.ShapeDtypeStruct((b, QK_WIDTH, s), BF16),
                      jax.ShapeDtypeStruct((b, s, v_width), BF16)]
        out_specs += [pl.BlockSpec((1, QK_WIDTH, tm), lambda i, j: (i, 0, j)), tile(v_width)]
    return pl.pallas_call(
        functools.partial(_proj_kernel, with_kv=kv is not None),
        out_shape=tuple(out_shape + cast_shapes),
        grid=(b, steps_per_seq),
        in_specs=specs,
        out_specs=tuple(out_specs + cast_specs),
        compiler_params=pltpu.CompilerParams(
            dimension_semantics=("arbitrary", "arbitrary"), vmem_limit_bytes=VMEM_LIMIT_BYTES),
        name="qkv_proj" if kv is not None else "q_proj",
    )(*args)


def _build_bias_tiles(rb_ref, bias_ref, t):
    row = lax.broadcasted_iota(jnp.int32, (t, t), 0)
    col = lax.broadcasted_iota(jnp.int32, (t, t), 1)
    for hh in range(N_HEADS):
        last = rb_ref[N_BUCKETS - 1, hh]
        for kind, offset in enumerate((0, t)):
            rel = row - col + offset
            tile = jnp.full((t, t), rb_ref[0, hh], F32)
            for bkt in range(1, N_BUCKETS):
                tile = jnp.where(rel >= BUCKET_START[bkt], rb_ref[bkt, hh], tile)
            tile = (tile - last) * LOG2E
            if offset == 0:
                tile = jnp.where(rel >= 0, tile, MASK_VALUE)
            bias_ref[hh, kind] = tile


def _attn_kernel(rb_ref, q_ref, kt_ref, v_ref, lam_ref, sub_ref, o_ref, bias_ref, vaug_ref,
                 *, lambda_init):
    t = ATTN_TILE
    seq = q_ref.shape[1]
    nq = seq // t
    qk, vd = 2 * HEAD_DIM, V_DIM
    head0 = pl.program_id(1) * HEADS_PER_STEP

    @pl.when((pl.program_id(0) == 0) & (pl.program_id(1) == 0))
    def _():
        _build_bias_tiles(rb_ref, bias_ref, t)

    lane_v = lax.broadcasted_iota(jnp.int32, (seq, vd), 1)
    for a in range(HEADS_PER_STEP):
        vaug_ref[a, :, :vd] = v_ref[0, :, a * vd:(a + 1) * vd]
        vaug_ref[a, :, vd:] = jnp.where(lane_v == 0, 1.0, 0.0).astype(BF16)

    lam_p = lam_ref[...]
    lam = (jnp.exp(jnp.sum(lam_p[0:1] * lam_p[1:2], axis=-1, keepdims=True))
           - jnp.exp(jnp.sum(lam_p[2:3] * lam_p[3:4], axis=-1, keepdims=True))
           + lambda_init)
    lane_q = lax.broadcasted_iota(jnp.int32, (t, 2 * HEAD_DIM), 1)

    def scores(a, i):
        q = q_ref[0, i * t:(i + 1) * t, a * qk:(a + 1) * qk]
        zero = jnp.zeros_like(q)
        q_stack = jnp.concatenate([jnp.where(lane_q < HEAD_DIM, q, zero),
                                   jnp.where(lane_q >= HEAD_DIM, q, zero)], axis=0)
        return _dot(q_stack, kt_ref[0, a * qk:(a + 1) * qk, :(i + 1) * t])

    def probabilities(a, i, s):
        hh = head0 + a
        ps = []
        for c in range(2):
            cols = [s[c * t:(c + 1) * t, kb * t:(kb + 1) * t] for kb in range(i + 1)]
            cols[i] = cols[i] + bias_ref[hh, 0]
            if i >= 1:
                cols[i - 1] = cols[i - 1] + bias_ref[hh, 1]
            sc = jnp.concatenate(cols, axis=1) if i >= 1 else cols[0]
            m = jnp.max(sc, axis=-1, keepdims=True)
            ps.append(jnp.exp2(sc - m).astype(BF16))
        return jnp.concatenate(ps, axis=0)

    def output(a, i, p):
        acc = _dot(p, vaug_ref[a, :(i + 1) * t, :])
        outs = [acc[c * t:(c + 1) * t, :vd] / acc[c * t:(c + 1) * t, vd:vd + 1]
                for c in range(2)]
        o = outs[0] - lam * outs[1]
        o = _rmsnorm(o, sub_ref[...]) * (1.0 - lambda_init)
        o_ref[0, i * t:(i + 1) * t, a * vd:(a + 1) * vd] = o.astype(BF16)

    order = [1] + list(range(nq - 1, 1, -1)) + [0]
    assert sorted(order) == list(range(nq))
    blocks = [(a, i) for i in order for a in range(HEADS_PER_STEP)]
    s_cur = p_cur = None
    for n in range(len(blocks) + 2):
        p_next = probabilities(*blocks[n - 1], s_cur) if 1 <= n <= len(blocks) else None
        s_next = scores(*blocks[n]) if n < len(blocks) else None
        if n >= 2:
            output(*blocks[n - 2], p_cur)
        s_cur, p_cur = s_next, p_next


def _diff_attention(rel_bias, q, kt, v, lam_params, subln, lambda_init):
    b, s, _ = q.shape
    t = ATTN_TILE
    hps = HEADS_PER_STEP
    assert t > FAR_DISTANCE and s % t == 0 and N_HEADS % hps == 0
    return pl.pallas_call(
        functools.partial(_attn_kernel, lambda_init=lambda_init),
        out_shape=jax.ShapeDtypeStruct((b, s, N_HEADS * V_DIM), BF16),
        grid=(b, N_HEADS // hps),
        in_specs=[
            pl.BlockSpec(memory_space=pltpu.SMEM),
            pl.BlockSpec((1, s, hps * 2 * HEAD_DIM), lambda i, j: (i, 0, j)),
            pl.BlockSpec((1, hps * 2 * HEAD_DIM, s), lambda i, j: (i, j, 0)),
            pl.BlockSpec((1, s, hps * V_DIM), lambda i, j: (i, 0, j)),
            pl.BlockSpec(lam_params.shape, lambda i, j: (0, 0)),
            pl.BlockSpec(subln.shape, lambda i, j: (0, 0)),
        ],
        out_specs=pl.BlockSpec((1, s, hps * V_DIM), lambda i, j: (i, 0, j)),
        scratch_shapes=[pltpu.VMEM((N_HEADS, 2, t, t), F32),
                        pltpu.VMEM((hps, s, 2 * V_DIM), BF16)],
        compiler_params=pltpu.CompilerParams(
            dimension_semantics=("arbitrary", "arbitrary"), vmem_limit_bytes=VMEM_LIMIT_BYTES),
        name="diff_attention",
    )(rel_bias, q, kt, v, lam_params, subln)


def kernel(x, norm_mix, norm_mlp, pool_w, pool_scale, kv_norm, w_kv, k_norm, rel_bias,
           w_q, q_norm, lam_q1, lam_k1, lam_q2, lam_k2, subln, w_o, w_mlp_in, w_mlp_out):
    b, s, d = x.shape
    n = b * s
    row = lambda a: a.reshape(1, -1).astype(F32)

    seg = np.arange(MXU_TILE) // HEAD_DIM
    avg = jnp.asarray((seg[:, None] == seg[None, :]).astype(np.float32) / HEAD_DIM, dtype=BF16)

    assert s % TOKEN_TILE == 0 and TOKEN_TILE >= MAX_WINDOW
    x2 = x.reshape(n, d)
    for l in range(N_A):
        x2 = _pool_mlp_layer(x2, row(norm_mix[l]), pool_w, row(pool_scale[l]), row(norm_mlp[l]),
                             w_mlp_in, w_mlp_out, l, s // TOKEN_TILE)

    kt = v = None
    for l in range(N_A, DEPTH):
        j = l - N_A
        lambda_init = 0.8 - 0.6 * math.exp(-0.3 * l)
        q_gain = row(jnp.tile(q_norm[j], QK_WIDTH // HEAD_DIM)) * (HEAD_DIM ** -0.5 * LOG2E)
        x3 = x2.reshape(b, s, d)
        cast = [(w_o, j), (w_mlp_in, l), (w_mlp_out, l)]
        if l == N_A:
            q, kt, v, wo_bf, win_bf, wout_bf = _projections(
                x3, row(norm_mix[l]), w_q, j, avg, q_gain, cast,
                kv=(row(kv_norm), w_kv, k_norm.reshape(HEAD_DIM, 1).astype(F32)))
        else:
            q, wo_bf, win_bf, wout_bf = _projections(x3, row(norm_mix[l]), w_q, j, avg, q_gain, cast)
        lam_params = jnp.stack([lam_q1[j], lam_k1[j], lam_q2[j], lam_k2[j]]).astype(F32)
        o = _diff_attention(rel_bias.astype(F32), q, kt, v, lam_params, row(subln[j]), lambda_init)
        x2 = _oproj_mlp_layer(x2, o.reshape(n, N_HEADS * V_DIM), wo_bf, row(norm_mlp[l]),
                              win_bf, wout_bf)
    return x2.reshape(b, s, d)
```

```python
import functools
import math

import numpy as np
import jax
import jax.numpy as jnp
from jax import lax
from jax.experimental import pallas as pl
from jax.experimental.pallas import tpu as pltpu

D_MODEL = 1024
DEPTH = 4
N_A = DEPTH // 2
POOL_WINDOWS = (2, 4, 8, 16)
N_GROUPS = len(POOL_WINDOWS)
GROUP_DIM = D_MODEL // N_GROUPS
N_HEADS = 8
HEAD_DIM = D_MODEL // (2 * N_HEADS)
V_DIM = 2 * HEAD_DIM
QK_WIDTH = N_HEADS * 2 * HEAD_DIM
D_FF = 4 * D_MODEL
N_BUCKETS = 32
MAX_DISTANCE = 128
EPS = 1e-6

LOG2E = math.log2(math.e)
MASK_VALUE = -1e30
MAX_WINDOW = max(POOL_WINDOWS)
LANES = 128
MXU_TILE = 256

VMEM_LIMIT_BYTES = 56 * 1024 * 1024

TOKEN_TILE = 512
FF_CHUNK = 1024
PROJ_TOKEN_TILE = 1024
OPROJ_TOKEN_TILE = 1024
ATTN_TILE = 256
HEADS_PER_STEP = 2

F32 = jnp.float32
BF16 = jnp.bfloat16


def _bucket_thresholds():
    max_exact = N_BUCKETS // 2
    n = np.arange(1, 4 * MAX_DISTANCE, dtype=np.float64)
    v = np.log(n / max_exact) / math.log(MAX_DISTANCE / max_exact) * (N_BUCKETS - max_exact)
    frac = v - np.floor(v)
    margin = np.minimum(frac, 1.0 - frac)[max_exact:MAX_DISTANCE - 1]
    assert margin.min() > 1e-3
    large = np.minimum(max_exact + v.astype(np.int64), N_BUCKETS - 1)
    bucket = np.where(n < max_exact, n.astype(np.int64), large)
    thr = [0] + [int(np.argmax(bucket >= b)) + 1 for b in range(1, N_BUCKETS)]
    return tuple(thr)


BUCKET_START = _bucket_thresholds()
FAR_DISTANCE = BUCKET_START[-1]


def _rmsnorm(xf, g):
    ms = jnp.mean(xf * xf, axis=-1, keepdims=True)
    return xf * lax.rsqrt(ms + EPS) * g


def _dot(a, b):
    return jnp.dot(a, b, preferred_element_type=F32)


def _pool_prepare(x_ref, seq_tile_idx, g_ref, carry_ref, ext_ref):
    t = x_ref.shape[0]
    h = _rmsnorm(x_ref[...], g_ref[...])
    ext_ref[:MAX_WINDOW, :] = jnp.where(seq_tile_idx == 0, 0.0, carry_ref[...])
    ext_ref[MAX_WINDOW:, :] = h
    carry_ref[...] = h[t - MAX_WINDOW:, :]


def _pool_group(g, x_ref, seq_tile_idx, pw_ref, ps_ref, ext_ref, out_ref):
    t = x_ref.shape[0]
    w = POOL_WINDOWS[g]
    lo, hi = g * GROUP_DIM, (g + 1) * GROUP_DIM
    s = ext_ref[:, lo:hi]
    h = s[MAX_WINDOW:, :]
    span = 1
    while span < w:
        s = s + pltpu.roll(s, span, axis=0)
        span *= 2
    pos = seq_tile_idx * t + lax.broadcasted_iota(jnp.int32, (t, 1), 0)
    cnt = jnp.minimum(pos + 1, w).astype(F32)
    m = s[MAX_WINDOW:, :] / cnt - h
    y = _dot(m.astype(BF16), pw_ref[g].astype(BF16))
    out_ref[:, lo:hi] = x_ref[:, lo:hi] + y * ps_ref[:, lo:hi]


def _pool_mlp_kernel(x0_ref, xn_ref, gmix_ref, pw_ref, ps_ref, g_ref, win_ref, wout_ref, out_ref,
                     carry_ref, ext_ref, mixed_ref, *, tiles_per_seq):
    k = pl.program_id(0)
    last = pl.num_programs(0) - 1

    @pl.when(k == 0)
    def _():
        _pool_prepare(x0_ref, 0, gmix_ref, carry_ref, ext_ref)
        for g in range(N_GROUPS):
            _pool_group(g, x0_ref, 0, pw_ref, ps_ref, ext_ref, mixed_ref.at[0])

    nxt = jnp.minimum(k + 1, last) % tiles_per_seq
    _pool_prepare(xn_ref, nxt, gmix_ref, carry_ref, ext_ref)
    side_jobs = [functools.partial(_pool_group, g, xn_ref, nxt, pw_ref, ps_ref, ext_ref,
                                   mixed_ref.at[(k + 1) % 2]) for g in range(N_GROUPS)]
    out_ref[...] = _mlp(mixed_ref[k % 2], g_ref, win_ref, wout_ref, side_jobs)


def _pool_mlp_layer(x2, g_mix, pw, ps, g, w_in, w_out, layer, tiles_per_seq):
    n, d = x2.shape
    tm = TOKEN_TILE
    last = n // tm - 1
    return pl.pallas_call(
        functools.partial(_pool_mlp_kernel, tiles_per_seq=tiles_per_seq),
        out_shape=jax.ShapeDtypeStruct(x2.shape, x2.dtype),
        grid=(n // tm,),
        in_specs=[
            pl.BlockSpec((tm, d), lambda i: (0, 0), pipeline_mode=pl.Buffered(1)),
            pl.BlockSpec((tm, d), lambda i: (jnp.minimum(i + 1, last), 0)),
            _resident(g_mix.shape), _resident_layer(pw, layer), _resident(ps.shape),
            _resident(g.shape), _resident_layer(w_in, layer), _resident_layer(w_out, layer),
        ],
        out_specs=pl.BlockSpec((tm, d), lambda i: (i, 0)),
        scratch_shapes=[pltpu.VMEM((MAX_WINDOW, d), F32),
                        pltpu.VMEM((MAX_WINDOW + tm, d), F32),
                        pltpu.VMEM((2, tm, d), F32)],
        compiler_params=pltpu.CompilerParams(
            dimension_semantics=("arbitrary",), vmem_limit_bytes=VMEM_LIMIT_BYTES),
        name="pool_mlp",
    )(x2, x2, g_mix, pw, ps, g, w_in, w_out)


def _mlp(x, g_ref, win_ref, wout_ref, side_jobs=()):
    h = _rmsnorm(x, g_ref[...]).astype(BF16)
    acc = x
    for c in range(D_FF // FF_CHUNK):
        lo, hi = c * FF_CHUNK, (c + 1) * FF_CHUNK
        a = _dot(h, win_ref[:, lo:hi].astype(BF16))
        if c < len(side_jobs):
            side_jobs[c]()
        a = jnp.square(jnp.maximum(a, 0.0)).astype(BF16)
        acc = acc + _dot(a, wout_ref[lo:hi, :].astype(BF16))
    return acc


def _oproj_mlp_kernel(x_ref, o_ref, wo_ref, g_ref, win_ref, wout_ref, out_ref):
    x = x_ref[...] + _dot(o_ref[...], wo_ref[...].astype(BF16))
    out_ref[...] = _mlp(x, g_ref, win_ref, wout_ref)


def _resident(shape):
    zeros = (0,) * len(shape)
    return pl.BlockSpec(shape, lambda *_: zeros, pipeline_mode=pl.Buffered(1))


def _resident_layer(stacked, layer):
    index = (layer,) + (0,) * (stacked.ndim - 1)
    return pl.BlockSpec((None,) + stacked.shape[1:], lambda *_: index,
                        pipeline_mode=pl.Buffered(1))


def _oproj_mlp_layer(x2, o2, w_o, g, w_in, w_out):
    n, d = x2.shape
    tm = OPROJ_TOKEN_TILE
    row = pl.BlockSpec((tm, d), lambda i: (i, 0))
    return pl.pallas_call(
        _oproj_mlp_kernel,
        out_shape=jax.ShapeDtypeStruct(x2.shape, x2.dtype),
        grid=(n // tm,),
        in_specs=[row, row, _resident(w_o.shape), _resident(g.shape),
                  _resident(w_in.shape), _resident(w_out.shape)],
        out_specs=row,
        compiler_params=pltpu.CompilerParams(
            dimension_semantics=("arbitrary",), vmem_limit_bytes=VMEM_LIMIT_BYTES),
        name="oproj_mlp",
    )(x2, o2, w_o, g, w_in, w_out)


def _proj_kernel(*refs, with_kv, n_cast):
    cast_in, refs = refs[:n_cast], refs[n_cast:]
    if with_kv:
        (x_ref, gq_ref, wq_ref, avg_ref, qg_ref, gkv_ref, wkv_ref, kn_ref,
         q_ref, kt_ref, v_ref, *cast_out) = refs
    else:
        x_ref, gq_ref, wq_ref, avg_ref, qg_ref, q_ref, *cast_out = refs
    for src, dst in zip(cast_in, cast_out):
        dst[...] = src[...].astype(BF16)
    x = x_ref[0]
    xn = x * lax.rsqrt(jnp.mean(x * x, axis=-1, keepdims=True) + EPS)

    q = _dot((xn * gq_ref[...]).astype(BF16), wq_ref[...].astype(BF16))
    sq = (q * q).astype(BF16)
    ms = jnp.concatenate([_dot(sq[:, k:k + MXU_TILE], avg_ref[...])
                          for k in range(0, QK_WIDTH, MXU_TILE)], axis=1)
    q_ref[0] = (q * lax.rsqrt(ms + EPS) * qg_ref[...]).astype(BF16)

    if with_kv:
        h = (xn * gkv_ref[...]).astype(BF16)
        w_k = wkv_ref[:, :QK_WIDTH].astype(BF16)
        kt = lax.dot_general(w_k, h, (((0,), (1,)), ((), ())),
                             preferred_element_type=F32)
        t = kt.shape[-1]
        kt = kt.reshape(QK_WIDTH // HEAD_DIM, HEAD_DIM, t)
        kms = jnp.mean(kt * kt, axis=1, keepdims=True)
        kt = kt * lax.rsqrt(kms + EPS) * kn_ref[...][None]
        kt_ref[0] = kt.reshape(QK_WIDTH, t).astype(BF16)
        v_ref[0] = _dot(h, wkv_ref[:, QK_WIDTH:].astype(BF16)).astype(BF16)


def _projections(x, g_q, w_q, layer, avg, q_gain, cast, kv=None):
    b, s, d = x.shape
    tm = PROJ_TOKEN_TILE
    steps_per_seq = s // tm
    n_steps = b * steps_per_seq
    tile = lambda width: pl.BlockSpec((1, tm, width), lambda i, j: (i, j, 0))
    args, specs, cast_shapes, cast_specs = [], [], [], []
    for w, w_layer in cast:
        rows, cols = w.shape[1] // n_steps, w.shape[2]
        assert w.shape[1] % n_steps == 0 and rows % 16 == 0
        args.append(w)
        specs.append(pl.BlockSpec((None, rows, cols),
                                  lambda i, j, w_layer=w_layer: (w_layer, i * steps_per_seq + j, 0)))
        cast_shapes.append(jax.ShapeDtypeStruct(w.shape[1:], BF16))
        cast_specs.append(pl.BlockSpec((rows, cols), lambda i, j: (i * steps_per_seq + j, 0)))
    args += [x, g_q, w_q, avg, q_gain]
    specs += [tile(d), _resident(g_q.shape), _resident_layer(w_q, layer), _resident(avg.shape),
              _resident(q_gain.shape)]
    out_shape = [jax.ShapeDtypeStruct((b, s, QK_WIDTH), BF16)]
    out_specs = [tile(QK_WIDTH)]
    if kv is not None:
        g_kv, w_kv, k_norm_col = kv
        v_width = w_kv.shape[1] - QK_WIDTH
        args += [g_kv, w_kv, k_norm_col]
        specs += [_resident(g_kv.shape), _resident(w_kv.shape), _resident(k_norm_col.shape)]
        out_shape += [jax.ShapeDtypeStruct((b, QK_WIDTH, s), BF16),
                      jax.ShapeDtypeStruct((b, s, v_width), BF16)]
        out_specs += [pl.BlockSpec((1, QK_WIDTH, tm), lambda i, j: (i, 0, j)), tile(v_width)]
    return pl.pallas_call(
        functools.partial(_proj_kernel, with_kv=kv is not None, n_cast=len(cast)),
        out_shape=tuple(out_shape + cast_shapes),
        grid=(b, steps_per_seq),
        in_specs=specs,
        out_specs=tuple(out_specs + cast_specs),
        compiler_params=pltpu.CompilerParams(
            dimension_semantics=("arbitrary", "arbitrary"), vmem_limit_bytes=VMEM_LIMIT_BYTES),
        name="qkv_proj" if kv is not None else "q_proj",
    )(*args)


def _build_bias_tiles(rb_ref, bias_ref, t):
    row = lax.broadcasted_iota(jnp.int32, (t, t), 0)
    col = lax.broadcasted_iota(jnp.int32, (t, t), 1)
    for hh in range(N_HEADS):
        last = rb_ref[N_BUCKETS - 1, hh]
        for kind, offset in enumerate((0, t)):
            rel = row - col + offset
            tile = jnp.full((t, t), rb_ref[0, hh], F32)
            for bkt in range(1, N_BUCKETS):
                tile = jnp.where(rel >= BUCKET_START[bkt], rb_ref[bkt, hh], tile)
            tile = (tile - last) * LOG2E
            if offset == 0:
                tile = jnp.where(rel >= 0, tile, MASK_VALUE)
            bias_ref[hh, kind] = tile


def _attn_kernel(rb_ref, q_ref, kt_ref, v_ref, lam_ref, sub_ref, o_ref, bias_ref, vaug_ref,
                 *, lambda_init):
    t = ATTN_TILE
    seq = q_ref.shape[1]
    nq = seq // t
    qk, vd = 2 * HEAD_DIM, V_DIM
    head0 = pl.program_id(1) * HEADS_PER_STEP

    @pl.when((pl.program_id(0) == 0) & (pl.program_id(1) == 0))
    def _():
        _build_bias_tiles(rb_ref, bias_ref, t)

    lane_v = lax.broadcasted_iota(jnp.int32, (seq, vd), 1)
    for a in range(HEADS_PER_STEP):
        vaug_ref[a, :, :vd] = v_ref[0, :, a * vd:(a + 1) * vd]
        vaug_ref[a, :, vd:] = jnp.where(lane_v == 0, 1.0, 0.0).astype(BF16)

    lam_p = lam_ref[...]
    lam = (jnp.exp(jnp.sum(lam_p[0:1] * lam_p[1:2], axis=-1, keepdims=True))
           - jnp.exp(jnp.sum(lam_p[2:3] * lam_p[3:4], axis=-1, keepdims=True))
           + lambda_init)
    lane_q = lax.broadcasted_iota(jnp.int32, (t, 2 * HEAD_DIM), 1)

    def scores(a, i):
        q = q_ref[0, i * t:(i + 1) * t, a * qk:(a + 1) * qk]
        zero = jnp.zeros_like(q)
        q_stack = jnp.concatenate([jnp.where(lane_q < HEAD_DIM, q, zero),
                                   jnp.where(lane_q >= HEAD_DIM, q, zero)], axis=0)
        return _dot(q_stack, kt_ref[0, a * qk:(a + 1) * qk, :(i + 1) * t])

    def probabilities(a, i, s):
        hh = head0 + a
        ps = []
        for c in range(2):
            cols = [s[c * t:(c + 1) * t, kb * t:(kb + 1) * t] for kb in range(i + 1)]
            cols[i] = cols[i] + bias_ref[hh, 0]
            if i >= 1:
                cols[i - 1] = cols[i - 1] + bias_ref[hh, 1]
            sc = jnp.concatenate(cols, axis=1) if i >= 1 else cols[0]
            m = jnp.max(sc, axis=-1, keepdims=True)
            ps.append(jnp.exp2(sc - m).astype(BF16))
        return jnp.concatenate(ps, axis=0)

    def output(a, i, p):
        acc = _dot(p, vaug_ref[a, :(i + 1) * t, :])
        outs = [acc[c * t:(c + 1) * t, :vd] / acc[c * t:(c + 1) * t, vd:vd + 1]
                for c in range(2)]
        o = outs[0] - lam * outs[1]
        o = _rmsnorm(o, sub_ref[...]) * (1.0 - lambda_init)
        o_ref[0, i * t:(i + 1) * t, a * vd:(a + 1) * vd] = o.astype(BF16)

    order = [1] + list(range(nq - 1, 1, -1)) + [0]
    assert sorted(order) == list(range(nq))
    blocks = [(a, i) for i in order for a in range(HEADS_PER_STEP)]
    s_cur = p_cur = None
    for n in range(len(blocks) + 2):
        p_next = probabilities(*blocks[n - 1], s_cur) if 1 <= n <= len(blocks) else None
        s_next = scores(*blocks[n]) if n < len(blocks) else None
        if n >= 2:
            output(*blocks[n - 2], p_cur)
        s_cur, p_cur = s_next, p_next


def _diff_attention(rel_bias, q, kt, v, lam_params, subln, lambda_init):
    b, s, _ = q.shape
    t = ATTN_TILE
    hps = HEADS_PER_STEP
    assert t > FAR_DISTANCE and s % t == 0 and N_HEADS % hps == 0
    return pl.pallas_call(
        functools.partial(_attn_kernel, lambda_init=lambda_init),
        out_shape=jax.ShapeDtypeStruct((b, s, N_HEADS * V_DIM), BF16),
        grid=(b, N_HEADS // hps),
        in_specs=[
            pl.BlockSpec(memory_space=pltpu.SMEM),
            pl.BlockSpec((1, s, hps * 2 * HEAD_DIM), lambda i, j: (i, 0, j)),
            pl.BlockSpec((1, hps * 2 * HEAD_DIM, s), lambda i, j: (i, j, 0)),
            pl.BlockSpec((1, s, hps * V_DIM), lambda i, j: (i, 0, j)),
            pl.BlockSpec(lam_params.shape, lambda i, j: (0, 0)),
            pl.BlockSpec(subln.shape, lambda i, j: (0, 0)),
        ],
        out_specs=pl.BlockSpec((1, s, hps * V_DIM), lambda i, j: (i, 0, j)),
        scratch_shapes=[pltpu.VMEM((N_HEADS, 2, t, t), F32),
                        pltpu.VMEM((hps, s, 2 * V_DIM), BF16)],
        compiler_params=pltpu.CompilerParams(
            dimension_semantics=("arbitrary", "arbitrary"), vmem_limit_bytes=VMEM_LIMIT_BYTES),
        name="diff_attention",
    )(rel_bias, q, kt, v, lam_params, subln)


def kernel(x, norm_mix, norm_mlp, pool_w, pool_scale, kv_norm, w_kv, k_norm, rel_bias,
           w_q, q_norm, lam_q1, lam_k1, lam_q2, lam_k2, subln, w_o, w_mlp_in, w_mlp_out):
    b, s, d = x.shape
    n = b * s
    row = lambda a: a.reshape(1, -1).astype(F32)

    seg = np.arange(MXU_TILE) // HEAD_DIM
    avg = jnp.asarray((seg[:, None] == seg[None, :]).astype(np.float32) / HEAD_DIM, dtype=BF16)

    assert s % TOKEN_TILE == 0 and TOKEN_TILE >= MAX_WINDOW
    x2 = x.reshape(n, d)
    for l in range(N_A):
        x2 = _pool_mlp_layer(x2, row(norm_mix[l]), pool_w, row(pool_scale[l]), row(norm_mlp[l]),
                             w_mlp_in, w_mlp_out, l, s // TOKEN_TILE)

    kt = v = None
    for l in range(N_A, DEPTH):
        j = l - N_A
        lambda_init = 0.8 - 0.6 * math.exp(-0.3 * l)
        q_gain = row(jnp.tile(q_norm[j], QK_WIDTH // HEAD_DIM)) * (HEAD_DIM ** -0.5 * LOG2E)
        x3 = x2.reshape(b, s, d)
        cast = [(w_o, j), (w_mlp_in, l), (w_mlp_out, l)]
        if l == N_A:
            q, kt, v, wo_bf, win_bf, wout_bf = _projections(
                x3, row(norm_mix[l]), w_q, j, avg, q_gain, cast,
                kv=(row(kv_norm), w_kv, k_norm.reshape(HEAD_DIM, 1).astype(F32)))
        else:
            q, wo_bf, win_bf, wout_bf = _projections(x3, row(norm_mix[l]), w_q, j, avg, q_gain, cast)
        lam_params = jnp.stack([lam_q1[j], lam_k1[j], lam_q2[j], lam_k2[j]]).astype(F32)
        o = _diff_attention(rel_bias.astype(F32), q, kt, v, lam_params, row(subln[j]), lambda_init)
        x2 = _oproj_mlp_layer(x2, o.reshape(n, N_HEADS * V_DIM), wo_bf, row(norm_mlp[l]),
                              win_bf, wout_bf)
    return x2.reshape(b, s, d)
```

```python
import functools
import math

import numpy as np
import jax
import jax.numpy as jnp
from jax import lax
from jax.experimental import pallas as pl
from jax.experimental.pallas import tpu as pltpu

D_MODEL = 1024
DEPTH = 4
N_A = DEPTH // 2
POOL_WINDOWS = (2, 4, 8, 16)
N_GROUPS = len(POOL_WINDOWS)
GROUP_DIM = D_MODEL // N_GROUPS
N_HEADS = 8
HEAD_DIM = D_MODEL // (2 * N_HEADS)
V_DIM = 2 * HEAD_DIM
QK_WIDTH = N_HEADS * 2 * HEAD_DIM
D_FF = 4 * D_MODEL
N_BUCKETS = 32
MAX_DISTANCE = 128
EPS = 1e-6

LOG2E = math.log2(math.e)
MASK_VALUE = -1e30
MAX_WINDOW = max(POOL_WINDOWS)
LANES = 128
MXU_TILE = 256

VMEM_LIMIT_BYTES = 56 * 1024 * 1024

TOKEN_TILE = 512
FF_CHUNK = 1024
PROJ_TOKEN_TILE = 1024
OPROJ_TOKEN_TILE = 1024
ATTN_TILE = 256
HEADS_PER_STEP = 2
VAUG_ROWS = V_DIM + 16

F32 = jnp.float32
BF16 = jnp.bfloat16


def _bucket_thresholds():
    max_exact = N_BUCKETS // 2
    n = np.arange(1, 4 * MAX_DISTANCE, dtype=np.float64)
    v = np.log(n / max_exact) / math.log(MAX_DISTANCE / max_exact) * (N_BUCKETS - max_exact)
    frac = v - np.floor(v)
    margin = np.minimum(frac, 1.0 - frac)[max_exact:MAX_DISTANCE - 1]
    assert margin.min() > 1e-3
    large = np.minimum(max_exact + v.astype(np.int64), N_BUCKETS - 1)
    bucket = np.where(n < max_exact, n.astype(np.int64), large)
    thr = [0] + [int(np.argmax(bucket >= b)) + 1 for b in range(1, N_BUCKETS)]
    return tuple(thr)


BUCKET_START = _bucket_thresholds()
FAR_DISTANCE = BUCKET_START[-1]


def _rmsnorm(xf, g):
    ms = jnp.mean(xf * xf, axis=-1, keepdims=True)
    return xf * lax.rsqrt(ms + EPS) * g


def _dot(a, b):
    return jnp.dot(a, b, preferred_element_type=F32)


def _pool_prepare(x_ref, seq_tile_idx, g_ref, carry_ref, ext_ref):
    t = x_ref.shape[0]
    h = _rmsnorm(x_ref[...], g_ref[...])
    ext_ref[:MAX_WINDOW, :] = jnp.where(seq_tile_idx == 0, 0.0, carry_ref[...])
    ext_ref[MAX_WINDOW:, :] = h
    carry_ref[...] = h[t - MAX_WINDOW:, :]


def _pool_group(g, x_ref, seq_tile_idx, pw_ref, ps_ref, ext_ref, out_ref):
    t = x_ref.shape[0]
    w = POOL_WINDOWS[g]
    lo, hi = g * GROUP_DIM, (g + 1) * GROUP_DIM
    s = ext_ref[:, lo:hi]
    h = s[MAX_WINDOW:, :]
    span = 1
    while span < w:
        s = s + pltpu.roll(s, span, axis=0)
        span *= 2
    pos = seq_tile_idx * t + lax.broadcasted_iota(jnp.int32, (t, 1), 0)
    cnt = jnp.minimum(pos + 1, w).astype(F32)
    m = s[MAX_WINDOW:, :] / cnt - h
    y = _dot(m.astype(BF16), pw_ref[g].astype(BF16))
    out_ref[:, lo:hi] = x_ref[:, lo:hi] + y * ps_ref[:, lo:hi]


def _pool_mlp_kernel(x0_ref, xn_ref, gmix_ref, pw_ref, ps_ref, g_ref, win_ref, wout_ref, out_ref,
                     carry_ref, ext_ref, mixed_ref, *, tiles_per_seq):
    k = pl.program_id(0)
    last = pl.num_programs(0) - 1

    @pl.when(k == 0)
    def _():
        _pool_prepare(x0_ref, 0, gmix_ref, carry_ref, ext_ref)
        for g in range(N_GROUPS):
            _pool_group(g, x0_ref, 0, pw_ref, ps_ref, ext_ref, mixed_ref.at[0])

    nxt = jnp.minimum(k + 1, last) % tiles_per_seq
    _pool_prepare(xn_ref, nxt, gmix_ref, carry_ref, ext_ref)
    side_jobs = [functools.partial(_pool_group, g, xn_ref, nxt, pw_ref, ps_ref, ext_ref,
                                   mixed_ref.at[(k + 1) % 2]) for g in range(N_GROUPS)]
    out_ref[...] = _mlp(mixed_ref[k % 2], g_ref, win_ref, wout_ref, side_jobs)


def _pool_mlp_layer(x2, g_mix, pw, ps, g, w_in, w_out, layer, tiles_per_seq):
    n, d = x2.shape
    tm = TOKEN_TILE
    last = n // tm - 1
    return pl.pallas_call(
        functools.partial(_pool_mlp_kernel, tiles_per_seq=tiles_per_seq),
        out_shape=jax.ShapeDtypeStruct(x2.shape, x2.dtype),
        grid=(n // tm,),
        in_specs=[
            pl.BlockSpec((tm, d), lambda i: (0, 0), pipeline_mode=pl.Buffered(1)),
            pl.BlockSpec((tm, d), lambda i: (jnp.minimum(i + 1, last), 0)),
            _resident(g_mix.shape), _resident_layer(pw, layer), _resident(ps.shape),
            _resident(g.shape), _resident_layer(w_in, layer), _resident_layer(w_out, layer),
        ],
        out_specs=pl.BlockSpec((tm, d), lambda i: (i, 0)),
        scratch_shapes=[pltpu.VMEM((MAX_WINDOW, d), F32),
                        pltpu.VMEM((MAX_WINDOW + tm, d), F32),
                        pltpu.VMEM((2, tm, d), F32)],
        compiler_params=pltpu.CompilerParams(
            dimension_semantics=("arbitrary",), vmem_limit_bytes=VMEM_LIMIT_BYTES),
        name="pool_mlp",
    )(x2, x2, g_mix, pw, ps, g, w_in, w_out)


def _mlp(x, g_ref, win_ref, wout_ref, side_jobs=()):
    h = _rmsnorm(x, g_ref[...]).astype(BF16)
    acc = x
    for c in range(D_FF // FF_CHUNK):
        lo, hi = c * FF_CHUNK, (c + 1) * FF_CHUNK
        a = _dot(h, win_ref[:, lo:hi].astype(BF16))
        if c < len(side_jobs):
            side_jobs[c]()
        a = jnp.square(jnp.maximum(a, 0.0)).astype(BF16)
        acc = acc + _dot(a, wout_ref[lo:hi, :].astype(BF16))
    return acc


def _oproj_mlp_kernel(x_ref, o_ref, wo_ref, g_ref, win_ref, wout_ref, out_ref):
    x = x_ref[...] + _dot(o_ref[...], wo_ref[...].astype(BF16))
    out_ref[...] = _mlp(x, g_ref, win_ref, wout_ref)


def _resident(shape):
    zeros = (0,) * len(shape)
    return pl.BlockSpec(shape, lambda *_: zeros, pipeline_mode=pl.Buffered(1))


def _resident_layer(stacked, layer):
    index = (layer,) + (0,) * (stacked.ndim - 1)
    return pl.BlockSpec((None,) + stacked.shape[1:], lambda *_: index,
                        pipeline_mode=pl.Buffered(1))


def _oproj_mlp_layer(x2, o2, w_o, g, w_in, w_out):
    n, d = x2.shape
    tm = OPROJ_TOKEN_TILE
    row = pl.BlockSpec((tm, d), lambda i: (i, 0))
    return pl.pallas_call(
        _oproj_mlp_kernel,
        out_shape=jax.ShapeDtypeStruct(x2.shape, x2.dtype),
        grid=(n // tm,),
        in_specs=[row, row, _resident(w_o.shape), _resident(g.shape),
                  _resident(w_in.shape), _resident(w_out.shape)],
        out_specs=row,
        compiler_params=pltpu.CompilerParams(
            dimension_semantics=("arbitrary",), vmem_limit_bytes=VMEM_LIMIT_BYTES),
        name="oproj_mlp",
    )(x2, o2, w_o, g, w_in, w_out)


def _proj_kernel(*refs, with_kv, n_cast):
    cast_in, refs = refs[:n_cast], refs[n_cast:]
    if with_kv:
        (x_ref, gq_ref, wq_ref, qn_ref, gkv_ref, wkv_ref, avg_ref, kg_ref,
         qt_ref, k_ref, vt_ref, *cast_out) = refs
    else:
        x_ref, gq_ref, wq_ref, qn_ref, qt_ref, *cast_out = refs
    for src, dst in zip(cast_in, cast_out):
        dst[...] = src[...].astype(BF16)
    x = x_ref[0]
    xn = x * lax.rsqrt(jnp.mean(x * x, axis=-1, keepdims=True) + EPS)

    def transposed_proj(w, h):
        return lax.dot_general(w, h, (((0,), (1,)), ((), ())), preferred_element_type=F32)

    qt = transposed_proj(wq_ref[...].astype(BF16), (xn * gq_ref[...]).astype(BF16))
    t = qt.shape[-1]
    qt = qt.reshape(QK_WIDTH // HEAD_DIM, HEAD_DIM, t)
    qms = jnp.mean(qt * qt, axis=1, keepdims=True)
    qt = qt * lax.rsqrt(qms + EPS) * qn_ref[...][None]
    qt_ref[0] = qt.reshape(QK_WIDTH, t).astype(BF16)

    if with_kv:
        h = (xn * gkv_ref[...]).astype(BF16)
        k = _dot(h, wkv_ref[:, :QK_WIDTH].astype(BF16))
        sq = (k * k).astype(BF16)
        kms = jnp.concatenate([_dot(sq[:, c:c + MXU_TILE], avg_ref[...])
                               for c in range(0, QK_WIDTH, MXU_TILE)], axis=1)
        k_ref[0] = (k * lax.rsqrt(kms + EPS) * kg_ref[...]).astype(BF16)
        vt_ref[0] = transposed_proj(wkv_ref[:, QK_WIDTH:].astype(BF16), h).astype(BF16)


def _projections(x, g_q, w_q, layer, q_norm_col, cast, kv=None):
    b, s, d = x.shape
    tm = PROJ_TOKEN_TILE
    steps_per_seq = s // tm
    n_steps = b * steps_per_seq
    tile = lambda width: pl.BlockSpec((1, tm, width), lambda i, j: (i, j, 0))
    args, specs, cast_shapes, cast_specs = [], [], [], []
    for w, w_layer in cast:
        rows, cols = w.shape[1] // n_steps, w.shape[2]
        assert w.shape[1] % n_steps == 0 and rows % 16 == 0
        args.append(w)
        specs.append(pl.BlockSpec((None, rows, cols),
                                  lambda i, j, w_layer=w_layer: (w_layer, i * steps_per_seq + j, 0)))
        cast_shapes.append(jax.ShapeDtypeStruct(w.shape[1:], BF16))
        cast_specs.append(pl.BlockSpec((rows, cols), lambda i, j: (i * steps_per_seq + j, 0)))
    tile_t = lambda height: pl.BlockSpec((1, height, tm), lambda i, j: (i, 0, j))
    args += [x, g_q, w_q, q_norm_col]
    specs += [tile(d), _resident(g_q.shape), _resident_layer(w_q, layer),
              _resident(q_norm_col.shape)]
    out_shape = [jax.ShapeDtypeStruct((b, QK_WIDTH, s), BF16)]
    out_specs = [tile_t(QK_WIDTH)]
    if kv is not None:
        g_kv, w_kv, avg, k_gain = kv
        v_width = w_kv.shape[1] - QK_WIDTH
        args += [g_kv, w_kv, avg, k_gain]
        specs += [_resident(g_kv.shape), _resident(w_kv.shape), _resident(avg.shape),
                  _resident(k_gain.shape)]
        out_shape += [jax.ShapeDtypeStruct((b, s, QK_WIDTH), BF16),
                      jax.ShapeDtypeStruct((b, v_width, s), BF16)]
        out_specs += [tile(QK_WIDTH), tile_t(v_width)]
    return pl.pallas_call(
        functools.partial(_proj_kernel, with_kv=kv is not None, n_cast=len(cast)),
        out_shape=tuple(out_shape + cast_shapes),
        grid=(b, steps_per_seq),
        in_specs=specs,
        out_specs=tuple(out_specs + cast_specs),
        compiler_params=pltpu.CompilerParams(
            dimension_semantics=("arbitrary", "arbitrary"), vmem_limit_bytes=VMEM_LIMIT_BYTES),
        name="qkv_proj" if kv is not None else "q_proj",
    )(*args)


def _build_bias_tiles(rb_ref, bias_ref, t):
    key = lax.broadcasted_iota(jnp.int32, (t, t), 0)
    query = lax.broadcasted_iota(jnp.int32, (t, t), 1)
    for hh in range(N_HEADS):
        last = rb_ref[N_BUCKETS - 1, hh]
        for kind, offset in enumerate((0, t)):
            rel = query - key + offset
            tile = jnp.full((t, t), rb_ref[0, hh], F32)
            for bkt in range(1, N_BUCKETS):
                tile = jnp.where(rel >= BUCKET_START[bkt], rb_ref[bkt, hh], tile)
            tile = (tile - last) * LOG2E
            if offset == 0:
                tile = jnp.where(rel >= 0, tile, MASK_VALUE)
            bias_ref[hh, kind] = tile


def _attn_kernel(rb_ref, qt_ref, k_ref, vt_ref, lam_ref, sub_ref, o_ref, bias_ref, vaug_ref,
                 *, lambda_init):
    t = ATTN_TILE
    seq = k_ref.shape[1]
    nq = seq // t
    qk, vd = 2 * HEAD_DIM, V_DIM
    head0 = pl.program_id(1) * HEADS_PER_STEP

    @pl.when((pl.program_id(0) == 0) & (pl.program_id(1) == 0))
    def _():
        _build_bias_tiles(rb_ref, bias_ref, t)

    pad_row = lax.broadcasted_iota(jnp.int32, (VAUG_ROWS - vd, seq), 0)
    for a in range(HEADS_PER_STEP):
        vaug_ref[a, :vd, :] = vt_ref[0, a * vd:(a + 1) * vd, :]
        vaug_ref[a, vd:, :] = jnp.where(pad_row == 0, 1.0, 0.0).astype(BF16)

    lam_p = lam_ref[...]
    lam = (jnp.exp(jnp.sum(lam_p[0:1] * lam_p[1:2], axis=-1, keepdims=True))
           - jnp.exp(jnp.sum(lam_p[2:3] * lam_p[3:4], axis=-1, keepdims=True))
           + lambda_init)
    feat = lax.broadcasted_iota(jnp.int32, (qk, t), 0)

    def scores(a, i):
        qt = qt_ref[0, a * qk:(a + 1) * qk, i * t:(i + 1) * t]
        zero = jnp.zeros_like(qt)
        q_stack = jnp.concatenate([jnp.where(feat < HEAD_DIM, qt, zero),
                                   jnp.where(feat >= HEAD_DIM, qt, zero)], axis=1)
        return _dot(k_ref[0, :(i + 1) * t, a * qk:(a + 1) * qk], q_stack)

    def probabilities(a, i, s):
        hh = head0 + a
        rows = [s[kb * t:(kb + 1) * t, :] for kb in range(i + 1)]
        both = lambda tile: jnp.concatenate([tile, tile], axis=1)
        rows[i] = rows[i] + both(bias_ref[hh, 0])
        if i >= 1:
            rows[i - 1] = rows[i - 1] + both(bias_ref[hh, 1])
        sc = jnp.concatenate(rows, axis=0) if i >= 1 else rows[0]
        m = jnp.max(sc, axis=0, keepdims=True)
        return jnp.exp2(sc - m).astype(BF16)

    def output(a, i, p):
        acc = _dot(vaug_ref[a, :, :(i + 1) * t], p)
        outs = [acc[:vd, c * t:(c + 1) * t] / acc[vd:vd + 1, c * t:(c + 1) * t] for c in range(2)]
        ot = outs[0] - lam * outs[1]
        ms = jnp.mean(ot * ot, axis=0, keepdims=True)
        ot = ot * lax.rsqrt(ms + EPS) * sub_ref[...] * (1.0 - lambda_init)
        o_ref[0, i * t:(i + 1) * t, a * vd:(a + 1) * vd] = ot.T.astype(BF16)

    order = [1] + list(range(nq - 1, 1, -1)) + [0]
    assert sorted(order) == list(range(nq))
    blocks = [(a, i) for i in order for a in range(HEADS_PER_STEP)]
    s_cur = p_cur = None
    for n in range(len(blocks) + 2):
        p_next = probabilities(*blocks[n - 1], s_cur) if 1 <= n <= len(blocks) else None
        s_next = scores(*blocks[n]) if n < len(blocks) else None
        if n >= 2:
            output(*blocks[n - 2], p_cur)
        s_cur, p_cur = s_next, p_next


def _diff_attention(rel_bias, qt, k, vt, lam_params, subln_col, lambda_init):
    b, s, _ = k.shape
    t = ATTN_TILE
    hps = HEADS_PER_STEP
    assert t > FAR_DISTANCE and s % t == 0 and N_HEADS % hps == 0
    rows_block = lambda width: pl.BlockSpec((1, s, width), lambda i, j: (i, 0, j))
    cols_block = lambda height: pl.BlockSpec((1, height, s), lambda i, j: (i, j, 0))
    return pl.pallas_call(
        functools.partial(_attn_kernel, lambda_init=lambda_init),
        out_shape=jax.ShapeDtypeStruct((b, s, N_HEADS * V_DIM), BF16),
        grid=(b, N_HEADS // hps),
        in_specs=[
            pl.BlockSpec(memory_space=pltpu.SMEM),
            cols_block(hps * 2 * HEAD_DIM),
            rows_block(hps * 2 * HEAD_DIM),
            cols_block(hps * V_DIM),
            pl.BlockSpec(lam_params.shape, lambda i, j: (0, 0)),
            pl.BlockSpec(subln_col.shape, lambda i, j: (0, 0)),
        ],
        out_specs=rows_block(hps * V_DIM),
        scratch_shapes=[pltpu.VMEM((N_HEADS, 2, t, t), F32),
                        pltpu.VMEM((hps, VAUG_ROWS, s), BF16)],
        compiler_params=pltpu.CompilerParams(
            dimension_semantics=("arbitrary", "arbitrary"), vmem_limit_bytes=VMEM_LIMIT_BYTES),
        name="diff_attention",
    )(rel_bias, qt, k, vt, lam_params, subln_col)


def kernel(x, norm_mix, norm_mlp, pool_w, pool_scale, kv_norm, w_kv, k_norm, rel_bias,
           w_q, q_norm, lam_q1, lam_k1, lam_q2, lam_k2, subln, w_o, w_mlp_in, w_mlp_out):
    b, s, d = x.shape
    n = b * s
    row = lambda a: a.reshape(1, -1).astype(F32)

    seg = np.arange(MXU_TILE) // HEAD_DIM
    avg = jnp.asarray((seg[:, None] == seg[None, :]).astype(np.float32) / HEAD_DIM, dtype=BF16)

    assert s % TOKEN_TILE == 0 and TOKEN_TILE >= MAX_WINDOW
    x2 = x.reshape(n, d)
    for l in range(N_A):
        x2 = _pool_mlp_layer(x2, row(norm_mix[l]), pool_w, row(pool_scale[l]), row(norm_mlp[l]),
                             w_mlp_in, w_mlp_out, l, s // TOKEN_TILE)

    k = vt = None
    for l in range(N_A, DEPTH):
        j = l - N_A
        lambda_init = 0.8 - 0.6 * math.exp(-0.3 * l)
        q_norm_col = (q_norm[j].astype(F32) * (HEAD_DIM ** -0.5 * LOG2E)).reshape(HEAD_DIM, 1)
        x3 = x2.reshape(b, s, d)
        cast = [(w_o, j), (w_mlp_in, l), (w_mlp_out, l)]
        if l == N_A:
            k_gain = row(jnp.tile(k_norm, QK_WIDTH // HEAD_DIM))
            qt, k, vt, wo_bf, win_bf, wout_bf = _projections(
                x3, row(norm_mix[l]), w_q, j, q_norm_col, cast,
                kv=(row(kv_norm), w_kv, avg, k_gain))
        else:
            qt, wo_bf, win_bf, wout_bf = _projections(x3, row(norm_mix[l]), w_q, j, q_norm_col, cast)
        lam_params = jnp.stack([lam_q1[j], lam_k1[j], lam_q2[j], lam_k2[j]]).astype(F32)
        o = _diff_attention(rel_bias.astype(F32), qt, k, vt, lam_params,
                            subln[j].astype(F32).reshape(V_DIM, 1), lambda_init)
        x2 = _oproj_mlp_layer(x2, o.reshape(n, N_HEADS * V_DIM), wo_bf, row(norm_mlp[l]),
                              win_bf, wout_bf)
    return x2.reshape(b, s, d)
```

```python
import functools
import math

import numpy as np
import jax
import jax.numpy as jnp
from jax import lax
from jax.experimental import pallas as pl
from jax.experimental.pallas import tpu as pltpu

D_MODEL = 1024
DEPTH = 4
N_A = DEPTH // 2
POOL_WINDOWS = (2, 4, 8, 16)
N_GROUPS = len(POOL_WINDOWS)
GROUP_DIM = D_MODEL // N_GROUPS
N_HEADS = 8
HEAD_DIM = D_MODEL // (2 * N_HEADS)
V_DIM = 2 * HEAD_DIM
QK_WIDTH = N_HEADS * 2 * HEAD_DIM
D_FF = 4 * D_MODEL
N_BUCKETS = 32
MAX_DISTANCE = 128
EPS = 1e-6

LOG2E = math.log2(math.e)
MASK_VALUE = -1e30
MAX_WINDOW = max(POOL_WINDOWS)
LANES = 128
MXU_TILE = 256

VMEM_LIMIT_BYTES = 56 * 1024 * 1024

TOKEN_TILE = 512
FF_CHUNK = 1024
PROJ_TOKEN_TILE = 1024
OPROJ_TOKEN_TILE = 1024
ATTN_TILE = 256
HEADS_PER_STEP = 2
VAUG_ROWS = V_DIM + 16

F32 = jnp.float32
BF16 = jnp.bfloat16


def _bucket_thresholds():
    max_exact = N_BUCKETS // 2
    n = np.arange(1, 4 * MAX_DISTANCE, dtype=np.float64)
    v = np.log(n / max_exact) / math.log(MAX_DISTANCE / max_exact) * (N_BUCKETS - max_exact)
    frac = v - np.floor(v)
    margin = np.minimum(frac, 1.0 - frac)[max_exact:MAX_DISTANCE - 1]
    assert margin.min() > 1e-3
    large = np.minimum(max_exact + v.astype(np.int64), N_BUCKETS - 1)
    bucket = np.where(n < max_exact, n.astype(np.int64), large)
    thr = [0] + [int(np.argmax(bucket >= b)) + 1 for b in range(1, N_BUCKETS)]
    return tuple(thr)


BUCKET_START = _bucket_thresholds()
FAR_DISTANCE = BUCKET_START[-1]


def _rmsnorm(xf, g):
    ms = jnp.mean(xf * xf, axis=-1, keepdims=True)
    return xf * lax.rsqrt(ms + EPS) * g


def _dot(a, b):
    return jnp.dot(a, b, preferred_element_type=F32)


def _pool_prepare(x_ref, seq_tile_idx, g_ref, carry_ref, ext_ref):
    t = x_ref.shape[0]
    h = _rmsnorm(x_ref[...], g_ref[...])
    ext_ref[:MAX_WINDOW, :] = jnp.where(seq_tile_idx == 0, 0.0, carry_ref[...])
    ext_ref[MAX_WINDOW:, :] = h
    carry_ref[...] = h[t - MAX_WINDOW:, :]


def _pool_group(g, x_ref, seq_tile_idx, pw_ref, ps_ref, ext_ref, out_ref):
    t = x_ref.shape[0]
    w = POOL_WINDOWS[g]
    lo, hi = g * GROUP_DIM, (g + 1) * GROUP_DIM
    s = ext_ref[:, lo:hi]
    h = s[MAX_WINDOW:, :]
    span = 1
    while span < w:
        s = s + pltpu.roll(s, span, axis=0)
        span *= 2
    pos = seq_tile_idx * t + lax.broadcasted_iota(jnp.int32, (t, 1), 0)
    cnt = jnp.minimum(pos + 1, w).astype(F32)
    m = s[MAX_WINDOW:, :] / cnt - h
    y = _dot(m.astype(BF16), pw_ref[g].astype(BF16))
    out_ref[:, lo:hi] = x_ref[:, lo:hi] + y * ps_ref[:, lo:hi]


def _pool_mlp_kernel(x0_ref, xn_ref, gmix_ref, pw_ref, ps_ref, g_ref, win_ref, wout_ref, out_ref,
                     carry_ref, ext_ref, mixed_ref, *, tiles_per_seq):
    k = pl.program_id(0)
    last = pl.num_programs(0) - 1

    @pl.when(k == 0)
    def _():
        _pool_prepare(x0_ref, 0, gmix_ref, carry_ref, ext_ref)
        for g in range(N_GROUPS):
            _pool_group(g, x0_ref, 0, pw_ref, ps_ref, ext_ref, mixed_ref.at[0])

    nxt = jnp.minimum(k + 1, last) % tiles_per_seq
    _pool_prepare(xn_ref, nxt, gmix_ref, carry_ref, ext_ref)
    side_jobs = [functools.partial(_pool_group, g, xn_ref, nxt, pw_ref, ps_ref, ext_ref,
                                   mixed_ref.at[(k + 1) % 2]) for g in range(N_GROUPS)]
    out_ref[...] = _mlp(mixed_ref[k % 2], g_ref, win_ref, wout_ref, side_jobs)


def _pool_mlp_layer(x2, g_mix, pw, ps, g, w_in, w_out, layer, tiles_per_seq):
    n, d = x2.shape
    tm = TOKEN_TILE
    last = n // tm - 1
    return pl.pallas_call(
        functools.partial(_pool_mlp_kernel, tiles_per_seq=tiles_per_seq),
        out_shape=jax.ShapeDtypeStruct(x2.shape, x2.dtype),
        grid=(n // tm,),
        in_specs=[
            pl.BlockSpec((tm, d), lambda i: (0, 0), pipeline_mode=pl.Buffered(1)),
            pl.BlockSpec((tm, d), lambda i: (jnp.minimum(i + 1, last), 0)),
            _resident(g_mix.shape), _resident_layer(pw, layer), _resident(ps.shape),
            _resident(g.shape), _resident_layer(w_in, layer), _resident_layer(w_out, layer),
        ],
        out_specs=pl.BlockSpec((tm, d), lambda i: (i, 0)),
        scratch_shapes=[pltpu.VMEM((MAX_WINDOW, d), F32),
                        pltpu.VMEM((MAX_WINDOW + tm, d), F32),
                        pltpu.VMEM((2, tm, d), F32)],
        compiler_params=pltpu.CompilerParams(
            dimension_semantics=("arbitrary",), vmem_limit_bytes=VMEM_LIMIT_BYTES),
        name="pool_mlp",
    )(x2, x2, g_mix, pw, ps, g, w_in, w_out)


def _mlp(x, g_ref, win_ref, wout_ref, side_jobs=()):
    h = _rmsnorm(x, g_ref[...]).astype(BF16)
    acc = x
    for c in range(D_FF // FF_CHUNK):
        lo, hi = c * FF_CHUNK, (c + 1) * FF_CHUNK
        a = _dot(h, win_ref[:, lo:hi].astype(BF16))
        if c < len(side_jobs):
            side_jobs[c]()
        a = jnp.square(jnp.maximum(a, 0.0)).astype(BF16)
        acc = acc + _dot(a, wout_ref[lo:hi, :].astype(BF16))
    return acc


def _oproj_mlp_kernel(x_ref, o_ref, wo_ref, g_ref, win_ref, wout_ref, out_ref):
    x = x_ref[...] + _dot(o_ref[...], wo_ref[...].astype(BF16))
    out_ref[...] = _mlp(x, g_ref, win_ref, wout_ref)


def _resident(shape):
    zeros = (0,) * len(shape)
    return pl.BlockSpec(shape, lambda *_: zeros, pipeline_mode=pl.Buffered(1))


def _resident_layer(stacked, layer):
    index = (layer,) + (0,) * (stacked.ndim - 1)
    return pl.BlockSpec((None,) + stacked.shape[1:], lambda *_: index,
                        pipeline_mode=pl.Buffered(1))


def _oproj_mlp_layer(x2, o2, w_o, g, w_in, w_out):
    n, d = x2.shape
    tm = OPROJ_TOKEN_TILE
    row = pl.BlockSpec((tm, d), lambda i: (i, 0))
    return pl.pallas_call(
        _oproj_mlp_kernel,
        out_shape=jax.ShapeDtypeStruct(x2.shape, x2.dtype),
        grid=(n // tm,),
        in_specs=[row, row, _resident(w_o.shape), _resident(g.shape),
                  _resident(w_in.shape), _resident(w_out.shape)],
        out_specs=row,
        compiler_params=pltpu.CompilerParams(
            dimension_semantics=("arbitrary",), vmem_limit_bytes=VMEM_LIMIT_BYTES),
        name="oproj_mlp",
    )(x2, o2, w_o, g, w_in, w_out)


def _proj_kernel(*refs, with_kv, n_cast):
    cast_in, refs = refs[:n_cast], refs[n_cast:]
    if with_kv:
        (x_ref, gq_ref, wq_ref, qn_ref, gkv_ref, wkv_ref, avg_ref, kg_ref,
         qt_ref, k_ref, vt_ref, *cast_out) = refs
    else:
        x_ref, gq_ref, wq_ref, qn_ref, qt_ref, *cast_out = refs
    for src, dst in zip(cast_in, cast_out):
        dst[...] = src[...].astype(BF16)
    x = x_ref[0]
    xn = x * lax.rsqrt(jnp.mean(x * x, axis=-1, keepdims=True) + EPS)

    def transposed_proj(w, h):
        return _dot(h, w).T

    qt = transposed_proj(wq_ref[...].astype(BF16), (xn * gq_ref[...]).astype(BF16))
    t = qt.shape[-1]
    qt = qt.reshape(QK_WIDTH // HEAD_DIM, HEAD_DIM, t)
    qms = jnp.mean(qt * qt, axis=1, keepdims=True)
    qt = qt * lax.rsqrt(qms + EPS) * qn_ref[...][None]
    qt_ref[0] = qt.reshape(QK_WIDTH, t).astype(BF16)

    if with_kv:
        h = (xn * gkv_ref[...]).astype(BF16)
        k = _dot(h, wkv_ref[:, :QK_WIDTH].astype(BF16))
        sq = (k * k).astype(BF16)
        kms = jnp.concatenate([_dot(sq[:, c:c + MXU_TILE], avg_ref[...])
                               for c in range(0, QK_WIDTH, MXU_TILE)], axis=1)
        k_ref[0] = (k * lax.rsqrt(kms + EPS) * kg_ref[...]).astype(BF16)
        vt_ref[0] = transposed_proj(wkv_ref[:, QK_WIDTH:].astype(BF16), h).astype(BF16)


def _projections(x, g_q, w_q, layer, q_norm_col, cast, kv=None):
    b, s, d = x.shape
    tm = PROJ_TOKEN_TILE
    steps_per_seq = s // tm
    n_steps = b * steps_per_seq
    tile = lambda width: pl.BlockSpec((1, tm, width), lambda i, j: (i, j, 0))
    args, specs, cast_shapes, cast_specs = [], [], [], []
    for w, w_layer in cast:
        rows, cols = w.shape[1] // n_steps, w.shape[2]
        assert w.shape[1] % n_steps == 0 and rows % 16 == 0
        args.append(w)
        specs.append(pl.BlockSpec((None, rows, cols),
                                  lambda i, j, w_layer=w_layer: (w_layer, i * steps_per_seq + j, 0)))
        cast_shapes.append(jax.ShapeDtypeStruct(w.shape[1:], BF16))
        cast_specs.append(pl.BlockSpec((rows, cols), lambda i, j: (i * steps_per_seq + j, 0)))
    tile_t = lambda height: pl.BlockSpec((1, height, tm), lambda i, j: (i, 0, j))
    args += [x, g_q, w_q, q_norm_col]
    specs += [tile(d), _resident(g_q.shape), _resident_layer(w_q, layer),
              _resident(q_norm_col.shape)]
    out_shape = [jax.ShapeDtypeStruct((b, QK_WIDTH, s), BF16)]
    out_specs = [tile_t(QK_WIDTH)]
    if kv is not None:
        g_kv, w_kv, avg, k_gain = kv
        v_width = w_kv.shape[1] - QK_WIDTH
        args += [g_kv, w_kv, avg, k_gain]
        specs += [_resident(g_kv.shape), _resident(w_kv.shape), _resident(avg.shape),
                  _resident(k_gain.shape)]
        out_shape += [jax.ShapeDtypeStruct((b, s, QK_WIDTH), BF16),
                      jax.ShapeDtypeStruct((b, v_width, s), BF16)]
        out_specs += [tile(QK_WIDTH), tile_t(v_width)]
    return pl.pallas_call(
        functools.partial(_proj_kernel, with_kv=kv is not None, n_cast=len(cast)),
        out_shape=tuple(out_shape + cast_shapes),
        grid=(b, steps_per_seq),
        in_specs=specs,
        out_specs=tuple(out_specs + cast_specs),
        compiler_params=pltpu.CompilerParams(
            dimension_semantics=("arbitrary", "arbitrary"), vmem_limit_bytes=VMEM_LIMIT_BYTES),
        name="qkv_proj" if kv is not None else "q_proj",
    )(*args)


def _build_bias_tiles(rb_ref, bias_ref, t):
    key = lax.broadcasted_iota(jnp.int32, (t, t), 0)
    query = lax.broadcasted_iota(jnp.int32, (t, t), 1)
    for hh in range(N_HEADS):
        last = rb_ref[N_BUCKETS - 1, hh]
        for kind, offset in enumerate((0, t)):
            rel = query - key + offset
            tile = jnp.full((t, t), rb_ref[0, hh], F32)
            for bkt in range(1, N_BUCKETS):
                tile = jnp.where(rel >= BUCKET_START[bkt], rb_ref[bkt, hh], tile)
            tile = (tile - last) * LOG2E
            if offset == 0:
                tile = jnp.where(rel >= 0, tile, MASK_VALUE)
            bias_ref[hh, kind] = tile


def _attn_kernel(rb_ref, qt_hbm, k_hbm, vt_hbm, lam_ref, sub_ref, o_hbm, bias_ref, vaug_ref,
                 *, lambda_init, grid, in_specs, out_specs):
    _build_bias_tiles(rb_ref, bias_ref, ATTN_TILE)
    step = functools.partial(_attn_step, lam_ref=lam_ref, sub_ref=sub_ref, bias_ref=bias_ref,
                             vaug_ref=vaug_ref, lambda_init=lambda_init)
    pltpu.emit_pipeline(step, grid=grid, in_specs=in_specs, out_specs=out_specs)(
        qt_hbm, k_hbm, vt_hbm, o_hbm)


def _attn_step(qt_ref, k_ref, vt_ref, o_ref, *, lam_ref, sub_ref, bias_ref, vaug_ref, lambda_init):
    t = ATTN_TILE
    seq = k_ref.shape[1]
    nq = seq // t
    qk, vd = 2 * HEAD_DIM, V_DIM
    head0 = pl.program_id(1) * HEADS_PER_STEP

    pad_row = lax.broadcasted_iota(jnp.int32, (VAUG_ROWS - vd, seq), 0)
    for a in range(HEADS_PER_STEP):
        vaug_ref[a, :vd, :] = vt_ref[0, a * vd:(a + 1) * vd, :]
        vaug_ref[a, vd:, :] = jnp.where(pad_row == 0, 1.0, 0.0).astype(BF16)

    lam_p = lam_ref[...]
    lam = (jnp.exp(jnp.sum(lam_p[0:1] * lam_p[1:2], axis=-1, keepdims=True))
           - jnp.exp(jnp.sum(lam_p[2:3] * lam_p[3:4], axis=-1, keepdims=True))
           + lambda_init)
    feat = lax.broadcasted_iota(jnp.int32, (qk, t), 0)

    def scores(a, i):
        qt = qt_ref[0, a * qk:(a + 1) * qk, i * t:(i + 1) * t]
        zero = jnp.zeros_like(qt)
        q_stack = jnp.concatenate([jnp.where(feat < HEAD_DIM, qt, zero),
                                   jnp.where(feat >= HEAD_DIM, qt, zero)], axis=1)
        return _dot(k_ref[0, :(i + 1) * t, a * qk:(a + 1) * qk], q_stack)

    def probabilities(a, i, s):
        hh = head0 + a
        rows = [s[kb * t:(kb + 1) * t, :] for kb in range(i + 1)]
        both = lambda tile: jnp.concatenate([tile, tile], axis=1)
        rows[i] = rows[i] + both(bias_ref[hh, 0])
        if i >= 1:
            rows[i - 1] = rows[i - 1] + both(bias_ref[hh, 1])
        sc = jnp.concatenate(rows, axis=0) if i >= 1 else rows[0]
        m = jnp.max(sc, axis=0, keepdims=True)
        return jnp.exp2(sc - m).astype(BF16)

    def output(a, i, p):
        acc = _dot(vaug_ref[a, :, :(i + 1) * t], p)
        outs = [acc[:vd, c * t:(c + 1) * t] / acc[vd:vd + 1, c * t:(c + 1) * t] for c in range(2)]
        ot = outs[0] - lam * outs[1]
        ms = jnp.mean(ot * ot, axis=0, keepdims=True)
        ot = ot * lax.rsqrt(ms + EPS) * sub_ref[...] * (1.0 - lambda_init)
        o_ref[0, i * t:(i + 1) * t, a * vd:(a + 1) * vd] = ot.T.astype(BF16)

    order = [1] + list(range(nq - 1, 1, -1)) + [0]
    assert sorted(order) == list(range(nq))
    blocks = [(a, i) for i in order for a in range(HEADS_PER_STEP)]
    s_cur = p_cur = None
    for n in range(len(blocks) + 2):
        p_next = probabilities(*blocks[n - 1], s_cur) if 1 <= n <= len(blocks) else None
        s_next = scores(*blocks[n]) if n < len(blocks) else None
        if n >= 2:
            output(*blocks[n - 2], p_cur)
        s_cur, p_cur = s_next, p_next


def _diff_attention(rel_bias, qt, k, vt, lam_params, subln_col, lambda_init):
    b, s, _ = k.shape
    t = ATTN_TILE
    hps = HEADS_PER_STEP
    assert t > FAR_DISTANCE and s % t == 0 and N_HEADS % hps == 0
    rows_block = lambda width: pl.BlockSpec((1, s, width), lambda i, j: (i, 0, j))
    cols_block = lambda height: pl.BlockSpec((1, height, s), lambda i, j: (i, j, 0))
    hbm = pl.BlockSpec(memory_space=pl.ANY)
    vmem = pl.BlockSpec(memory_space=pltpu.VMEM)
    return pl.pallas_call(
        functools.partial(
            _attn_kernel, lambda_init=lambda_init, grid=(b, N_HEADS // hps),
            in_specs=[cols_block(hps * 2 * HEAD_DIM), rows_block(hps * 2 * HEAD_DIM),
                      cols_block(hps * V_DIM)],
            out_specs=[rows_block(hps * V_DIM)]),
        out_shape=jax.ShapeDtypeStruct((b, s, N_HEADS * V_DIM), BF16),
        in_specs=[pl.BlockSpec(memory_space=pltpu.SMEM), hbm, hbm, hbm, vmem, vmem],
        out_specs=hbm,
        scratch_shapes=[pltpu.VMEM((N_HEADS, 2, t, t), F32),
                        pltpu.VMEM((hps, VAUG_ROWS, s), BF16)],
        compiler_params=pltpu.CompilerParams(vmem_limit_bytes=VMEM_LIMIT_BYTES),
        name="diff_attention",
    )(rel_bias, qt, k, vt, lam_params, subln_col)


def kernel(x, norm_mix, norm_mlp, pool_w, pool_scale, kv_norm, w_kv, k_norm, rel_bias,
           w_q, q_norm, lam_q1, lam_k1, lam_q2, lam_k2, subln, w_o, w_mlp_in, w_mlp_out):
    b, s, d = x.shape
    n = b * s
    row = lambda a: a.reshape(1, -1).astype(F32)

    seg = np.arange(MXU_TILE) // HEAD_DIM
    avg = jnp.asarray((seg[:, None] == seg[None, :]).astype(np.float32) / HEAD_DIM, dtype=BF16)

    assert s % TOKEN_TILE == 0 and TOKEN_TILE >= MAX_WINDOW
    x2 = x.reshape(n, d)
    for l in range(N_A):
        x2 = _pool_mlp_layer(x2, row(norm_mix[l]), pool_w, row(pool_scale[l]), row(norm_mlp[l]),
                             w_mlp_in, w_mlp_out, l, s // TOKEN_TILE)

    k = vt = None
    for l in range(N_A, DEPTH):
        j = l - N_A
        lambda_init = 0.8 - 0.6 * math.exp(-0.3 * l)
        q_norm_col = (q_norm[j].astype(F32) * (HEAD_DIM ** -0.5 * LOG2E)).reshape(HEAD_DIM, 1)
        x3 = x2.reshape(b, s, d)
        cast = [(w_o, j), (w_mlp_in, l), (w_mlp_out, l)]
        if l == N_A:
            k_gain = row(jnp.tile(k_norm, QK_WIDTH // HEAD_DIM))
            qt, k, vt, wo_bf, win_bf, wout_bf = _projections(
                x3, row(norm_mix[l]), w_q, j, q_norm_col, cast,
                kv=(row(kv_norm), w_kv, avg, k_gain))
        else:
            qt, wo_bf, win_bf, wout_bf = _projections(x3, row(norm_mix[l]), w_q, j, q_norm_col, cast)
        lam_params = jnp.stack([lam_q1[j], lam_k1[j], lam_q2[j], lam_k2[j]]).astype(F32)
        o = _diff_attention(rel_bias.astype(F32), qt, k, vt, lam_params,
                            subln[j].astype(F32).reshape(V_DIM, 1), lambda_init)
        x2 = _oproj_mlp_layer(x2, o.reshape(n, N_HEADS * V_DIM), wo_bf, row(norm_mlp[l]),
                              win_bf, wout_bf)
    return x2.reshape(b, s, d)
```

```python
import functools
import math

import numpy as np
import jax
import jax.numpy as jnp
from jax import lax
from jax.experimental import pallas as pl
from jax.experimental.pallas import tpu as pltpu

D_MODEL = 1024
DEPTH = 4
N_A = DEPTH // 2
POOL_WINDOWS = (2, 4, 8, 16)
N_GROUPS = len(POOL_WINDOWS)
GROUP_DIM = D_MODEL // N_GROUPS
N_HEADS = 8
HEAD_DIM = D_MODEL // (2 * N_HEADS)
V_DIM = 2 * HEAD_DIM
QK_WIDTH = N_HEADS * 2 * HEAD_DIM
D_FF = 4 * D_MODEL
N_BUCKETS = 32
MAX_DISTANCE = 128
EPS = 1e-6

LOG2E = math.log2(math.e)
MASK_VALUE = -1e30
MAX_WINDOW = max(POOL_WINDOWS)
LANES = 128
MXU_TILE = 256

VMEM_LIMIT_BYTES = 56 * 1024 * 1024

TOKEN_TILE = 512
FF_CHUNK = 1024
PROJ_TOKEN_TILE = 1024
OPROJ_TOKEN_TILE = 1024
ATTN_TILE = 256
HEADS_PER_STEP = 2
VAUG_ROWS = V_DIM + 16

F32 = jnp.float32
BF16 = jnp.bfloat16


def _bucket_thresholds():
    max_exact = N_BUCKETS // 2
    n = np.arange(1, 4 * MAX_DISTANCE, dtype=np.float64)
    v = np.log(n / max_exact) / math.log(MAX_DISTANCE / max_exact) * (N_BUCKETS - max_exact)
    frac = v - np.floor(v)
    margin = np.minimum(frac, 1.0 - frac)[max_exact:MAX_DISTANCE - 1]
    assert margin.min() > 1e-3
    large = np.minimum(max_exact + v.astype(np.int64), N_BUCKETS - 1)
    bucket = np.where(n < max_exact, n.astype(np.int64), large)
    thr = [0] + [int(np.argmax(bucket >= b)) + 1 for b in range(1, N_BUCKETS)]
    return tuple(thr)


BUCKET_START = _bucket_thresholds()
FAR_DISTANCE = BUCKET_START[-1]


def _rmsnorm(xf, g):
    ms = jnp.mean(xf * xf, axis=-1, keepdims=True)
    return xf * lax.rsqrt(ms + EPS) * g


def _dot(a, b):
    return jnp.dot(a, b, preferred_element_type=F32)


def _pool_prepare(x_ref, seq_tile_idx, g_ref, carry_ref, ext_ref):
    t = x_ref.shape[0]
    h = _rmsnorm(x_ref[...], g_ref[...])
    ext_ref[:MAX_WINDOW, :] = jnp.where(seq_tile_idx == 0, 0.0, carry_ref[...])
    ext_ref[MAX_WINDOW:, :] = h
    carry_ref[...] = h[t - MAX_WINDOW:, :]


def _pool_group(g, x_ref, seq_tile_idx, pw_ref, ps_ref, ext_ref, out_ref):
    t = x_ref.shape[0]
    w = POOL_WINDOWS[g]
    lo, hi = g * GROUP_DIM, (g + 1) * GROUP_DIM
    s = ext_ref[:, lo:hi]
    h = s[MAX_WINDOW:, :]
    span = 1
    while span < w:
        s = s + pltpu.roll(s, span, axis=0)
        span *= 2
    pos = seq_tile_idx * t + lax.broadcasted_iota(jnp.int32, (t, 1), 0)
    cnt = jnp.minimum(pos + 1, w).astype(F32)
    m = s[MAX_WINDOW:, :] / cnt - h
    y = _dot(m.astype(BF16), pw_ref[g].astype(BF16))
    out_ref[:, lo:hi] = x_ref[:, lo:hi] + y * ps_ref[:, lo:hi]


def _pool_mlp_kernel(x0_ref, xn_ref, gmix_ref, pw_ref, ps_ref, g_ref, win_ref, wout_ref, out_ref,
                     carry_ref, ext_ref, mixed_ref, *, tiles_per_seq):
    k = pl.program_id(0)
    last = pl.num_programs(0) - 1

    @pl.when(k == 0)
    def _():
        _pool_prepare(x0_ref, 0, gmix_ref, carry_ref, ext_ref)
        for g in range(N_GROUPS):
            _pool_group(g, x0_ref, 0, pw_ref, ps_ref, ext_ref, mixed_ref.at[0])

    nxt = jnp.minimum(k + 1, last) % tiles_per_seq
    _pool_prepare(xn_ref, nxt, gmix_ref, carry_ref, ext_ref)
    side_jobs = [functools.partial(_pool_group, g, xn_ref, nxt, pw_ref, ps_ref, ext_ref,
                                   mixed_ref.at[(k + 1) % 2]) for g in range(N_GROUPS)]
    out_ref[...] = _mlp(mixed_ref[k % 2], g_ref, win_ref, wout_ref, side_jobs)


def _pool_mlp_layer(x2, g_mix, pw, ps, g, w_in, w_out, layer, tiles_per_seq):
    n, d = x2.shape
    tm = TOKEN_TILE
    last = n // tm - 1
    return pl.pallas_call(
        functools.partial(_pool_mlp_kernel, tiles_per_seq=tiles_per_seq),
        out_shape=jax.ShapeDtypeStruct(x2.shape, x2.dtype),
        grid=(n // tm,),
        in_specs=[
            pl.BlockSpec((tm, d), lambda i: (0, 0), pipeline_mode=pl.Buffered(1)),
            pl.BlockSpec((tm, d), lambda i: (jnp.minimum(i + 1, last), 0)),
            _resident(g_mix.shape), _resident_layer(pw, layer), _resident(ps.shape),
            _resident(g.shape), _resident_layer(w_in, layer), _resident_layer(w_out, layer),
        ],
        out_specs=pl.BlockSpec((tm, d), lambda i: (i, 0)),
        scratch_shapes=[pltpu.VMEM((MAX_WINDOW, d), F32),
                        pltpu.VMEM((MAX_WINDOW + tm, d), F32),
                        pltpu.VMEM((2, tm, d), F32)],
        compiler_params=pltpu.CompilerParams(
            dimension_semantics=("arbitrary",), vmem_limit_bytes=VMEM_LIMIT_BYTES),
        name="pool_mlp",
    )(x2, x2, g_mix, pw, ps, g, w_in, w_out)


def _mlp(x, g_ref, win_ref, wout_ref, side_jobs=()):
    h = _rmsnorm(x, g_ref[...]).astype(BF16)
    acc = x
    for c in range(D_FF // FF_CHUNK):
        lo, hi = c * FF_CHUNK, (c + 1) * FF_CHUNK
        a = _dot(h, win_ref[:, lo:hi].astype(BF16))
        if c < len(side_jobs):
            side_jobs[c]()
        a = jnp.square(jnp.maximum(a, 0.0)).astype(BF16)
        acc = acc + _dot(a, wout_ref[lo:hi, :].astype(BF16))
    return acc


def _oproj_mlp_kernel(x_hbm, o_hbm, wo_ref, g_ref, win_ref, wout_ref, out_hbm, *, grid, row):
    def step(x_ref, o_ref, out_ref):
        x = x_ref[...] + _dot(o_ref[...], wo_ref[...].astype(BF16))
        out_ref[...] = _mlp(x, g_ref, win_ref, wout_ref)

    pltpu.emit_pipeline(step, grid=grid, in_specs=[row, row], out_specs=[row])(
        x_hbm, o_hbm, out_hbm)


def _resident(shape):
    zeros = (0,) * len(shape)
    return pl.BlockSpec(shape, lambda *_: zeros, pipeline_mode=pl.Buffered(1))


def _resident_layer(stacked, layer):
    index = (layer,) + (0,) * (stacked.ndim - 1)
    return pl.BlockSpec((None,) + stacked.shape[1:], lambda *_: index,
                        pipeline_mode=pl.Buffered(1))


def _oproj_mlp_layer(x2, o2, w_o, g, w_in, w_out):
    n, d = x2.shape
    tm = OPROJ_TOKEN_TILE
    row = pl.BlockSpec((tm, d), lambda i: (i, 0))
    hbm = pl.BlockSpec(memory_space=pl.ANY)
    vmem = pl.BlockSpec(memory_space=pltpu.VMEM)
    return pl.pallas_call(
        functools.partial(_oproj_mlp_kernel, grid=(n // tm,), row=row),
        out_shape=jax.ShapeDtypeStruct(x2.shape, x2.dtype),
        in_specs=[hbm, hbm, vmem, vmem, vmem, vmem],
        out_specs=hbm,
        compiler_params=pltpu.CompilerParams(vmem_limit_bytes=VMEM_LIMIT_BYTES),
        name="oproj_mlp",
    )(x2, o2, w_o, g, w_in, w_out)


def _proj_kernel(*refs, with_kv, n_cast):
    cast_in, refs = refs[:n_cast], refs[n_cast:]
    if with_kv:
        (x_ref, gq_ref, wq_ref, qn_ref, gkv_ref, wkv_ref, avg_ref, kg_ref,
         qt_ref, k_ref, vt_ref, *cast_out) = refs
    else:
        x_ref, gq_ref, wq_ref, qn_ref, qt_ref, *cast_out = refs
    for src, dst in zip(cast_in, cast_out):
        dst[...] = src[...].astype(BF16)
    x = x_ref[0]
    xn = x * lax.rsqrt(jnp.mean(x * x, axis=-1, keepdims=True) + EPS)

    def transposed_proj(w, h):
        return _dot(h, w).T

    qt = transposed_proj(wq_ref[...].astype(BF16), (xn * gq_ref[...]).astype(BF16))
    t = qt.shape[-1]
    qt = qt.reshape(QK_WIDTH // HEAD_DIM, HEAD_DIM, t)
    qms = jnp.mean(qt * qt, axis=1, keepdims=True)
    qt = qt * lax.rsqrt(qms + EPS) * qn_ref[...][None]
    qt_ref[0] = qt.reshape(QK_WIDTH, t).astype(BF16)

    if with_kv:
        h = (xn * gkv_ref[...]).astype(BF16)
        k = _dot(h, wkv_ref[:, :QK_WIDTH].astype(BF16))
        sq = (k * k).astype(BF16)
        kms = jnp.concatenate([_dot(sq[:, c:c + MXU_TILE], avg_ref[...])
                               for c in range(0, QK_WIDTH, MXU_TILE)], axis=1)
        k_ref[0] = (k * lax.rsqrt(kms + EPS) * kg_ref[...]).astype(BF16)
        vt_ref[0] = transposed_proj(wkv_ref[:, QK_WIDTH:].astype(BF16), h).astype(BF16)


def _projections(x, g_q, w_q, layer, q_norm_col, cast, kv=None):
    b, s, d = x.shape
    tm = PROJ_TOKEN_TILE
    steps_per_seq = s // tm
    n_steps = b * steps_per_seq
    tile = lambda width: pl.BlockSpec((1, tm, width), lambda i, j: (i, j, 0))
    args, specs, cast_shapes, cast_specs = [], [], [], []
    for w, w_layer in cast:
        rows, cols = w.shape[1] // n_steps, w.shape[2]
        assert w.shape[1] % n_steps == 0 and rows % 16 == 0
        args.append(w)
        specs.append(pl.BlockSpec((None, rows, cols),
                                  lambda i, j, w_layer=w_layer: (w_layer, i * steps_per_seq + j, 0)))
        cast_shapes.append(jax.ShapeDtypeStruct(w.shape[1:], BF16))
        cast_specs.append(pl.BlockSpec((rows, cols), lambda i, j: (i * steps_per_seq + j, 0)))
    tile_t = lambda height: pl.BlockSpec((1, height, tm), lambda i, j: (i, 0, j))
    args += [x, g_q, w_q, q_norm_col]
    specs += [tile(d), _resident(g_q.shape), _resident_layer(w_q, layer),
              _resident(q_norm_col.shape)]
    out_shape = [jax.ShapeDtypeStruct((b, QK_WIDTH, s), BF16)]
    out_specs = [tile_t(QK_WIDTH)]
    if kv is not None:
        g_kv, w_kv, avg, k_gain = kv
        v_width = w_kv.shape[1] - QK_WIDTH
        args += [g_kv, w_kv, avg, k_gain]
        specs += [_resident(g_kv.shape), _resident(w_kv.shape), _resident(avg.shape),
                  _resident(k_gain.shape)]
        out_shape += [jax.ShapeDtypeStruct((b, s, QK_WIDTH), BF16),
                      jax.ShapeDtypeStruct((b, v_width, s), BF16)]
        out_specs += [tile(QK_WIDTH), tile_t(v_width)]
    return pl.pallas_call(
        functools.partial(_proj_kernel, with_kv=kv is not None, n_cast=len(cast)),
        out_shape=tuple(out_shape + cast_shapes),
        grid=(b, steps_per_seq),
        in_specs=specs,
        out_specs=tuple(out_specs + cast_specs),
        compiler_params=pltpu.CompilerParams(
            dimension_semantics=("arbitrary", "arbitrary"), vmem_limit_bytes=VMEM_LIMIT_BYTES),
        name="qkv_proj" if kv is not None else "q_proj",
    )(*args)


def _build_bias_tiles(rb_ref, bias_ref, t):
    key = lax.broadcasted_iota(jnp.int32, (t, t), 0)
    query = lax.broadcasted_iota(jnp.int32, (t, t), 1)
    for hh in range(N_HEADS):
        last = rb_ref[N_BUCKETS - 1, hh]
        for kind, offset in enumerate((0, t)):
            rel = query - key + offset
            tile = jnp.full((t, t), rb_ref[0, hh], F32)
            for bkt in range(1, N_BUCKETS):
                tile = jnp.where(rel >= BUCKET_START[bkt], rb_ref[bkt, hh], tile)
            tile = (tile - last) * LOG2E
            if offset == 0:
                tile = jnp.where(rel >= 0, tile, MASK_VALUE)
            bias_ref[hh, kind] = tile


def _attn_kernel(rb_ref, qt_hbm, k_hbm, vt_hbm, lam_ref, sub_ref, o_hbm, bias_ref, vaug_ref,
                 *, lambda_init, grid, in_specs, out_specs):
    _build_bias_tiles(rb_ref, bias_ref, ATTN_TILE)
    step = functools.partial(_attn_step, lam_ref=lam_ref, sub_ref=sub_ref, bias_ref=bias_ref,
                             vaug_ref=vaug_ref, lambda_init=lambda_init)
    pltpu.emit_pipeline(step, grid=grid, in_specs=in_specs, out_specs=out_specs)(
        qt_hbm, k_hbm, vt_hbm, o_hbm)


def _attn_step(qt_ref, k_ref, vt_ref, o_ref, *, lam_ref, sub_ref, bias_ref, vaug_ref, lambda_init):
    t = ATTN_TILE
    seq = k_ref.shape[1]
    nq = seq // t
    qk, vd = 2 * HEAD_DIM, V_DIM
    head0 = pl.program_id(1) * HEADS_PER_STEP

    pad_row = lax.broadcasted_iota(jnp.int32, (VAUG_ROWS - vd, seq), 0)
    for a in range(HEADS_PER_STEP):
        vaug_ref[a, :vd, :] = vt_ref[0, a * vd:(a + 1) * vd, :]
        vaug_ref[a, vd:, :] = jnp.where(pad_row == 0, 1.0, 0.0).astype(BF16)

    lam_p = lam_ref[...]
    lam = (jnp.exp(jnp.sum(lam_p[0:1] * lam_p[1:2], axis=-1, keepdims=True))
           - jnp.exp(jnp.sum(lam_p[2:3] * lam_p[3:4], axis=-1, keepdims=True))
           + lambda_init)
    feat = lax.broadcasted_iota(jnp.int32, (qk, t), 0)

    def scores(a, i):
        qt = qt_ref[0, a * qk:(a + 1) * qk, i * t:(i + 1) * t]
        zero = jnp.zeros_like(qt)
        q_stack = jnp.concatenate([jnp.where(feat < HEAD_DIM, qt, zero),
                                   jnp.where(feat >= HEAD_DIM, qt, zero)], axis=1)
        return _dot(k_ref[0, :(i + 1) * t, a * qk:(a + 1) * qk], q_stack)

    def probabilities(a, i, s):
        hh = head0 + a
        rows = [s[kb * t:(kb + 1) * t, :] for kb in range(i + 1)]
        both = lambda tile: jnp.concatenate([tile, tile], axis=1)
        rows[i] = rows[i] + both(bias_ref[hh, 0])
        if i >= 1:
            rows[i - 1] = rows[i - 1] + both(bias_ref[hh, 1])
        sc = jnp.concatenate(rows, axis=0) if i >= 1 else rows[0]
        m = jnp.max(sc, axis=0, keepdims=True)
        return jnp.exp2(sc - m).astype(BF16)

    def output(a, i, p):
        acc = _dot(vaug_ref[a, :, :(i + 1) * t], p)
        outs = [acc[:vd, c * t:(c + 1) * t] / acc[vd:vd + 1, c * t:(c + 1) * t] for c in range(2)]
        ot = outs[0] - lam * outs[1]
        ms = jnp.mean(ot * ot, axis=0, keepdims=True)
        ot = ot * lax.rsqrt(ms + EPS) * sub_ref[...] * (1.0 - lambda_init)
        o_ref[0, i * t:(i + 1) * t, a * vd:(a + 1) * vd] = ot.T.astype(BF16)

    order = [1] + list(range(nq - 1, 1, -1)) + [0]
    assert sorted(order) == list(range(nq))
    blocks = [(a, i) for i in order for a in range(HEADS_PER_STEP)]
    s_cur = p_cur = None
    for n in range(len(blocks) + 2):
        p_next = probabilities(*blocks[n - 1], s_cur) if 1 <= n <= len(blocks) else None
        s_next = scores(*blocks[n]) if n < len(blocks) else None
        if n >= 2:
            output(*blocks[n - 2], p_cur)
        s_cur, p_cur = s_next, p_next


def _diff_attention(rel_bias, qt, k, vt, lam_params, subln_col, lambda_init):
    b, s, _ = k.shape
    t = ATTN_TILE
    hps = HEADS_PER_STEP
    assert t > FAR_DISTANCE and s % t == 0 and N_HEADS % hps == 0
    rows_block = lambda width: pl.BlockSpec((1, s, width), lambda i, j: (i, 0, j))
    cols_block = lambda height: pl.BlockSpec((1, height, s), lambda i, j: (i, j, 0))
    hbm = pl.BlockSpec(memory_space=pl.ANY)
    vmem = pl.BlockSpec(memory_space=pltpu.VMEM)
    return pl.pallas_call(
        functools.partial(
            _attn_kernel, lambda_init=lambda_init, grid=(b, N_HEADS // hps),
            in_specs=[cols_block(hps * 2 * HEAD_DIM), rows_block(hps * 2 * HEAD_DIM),
                      cols_block(hps * V_DIM)],
            out_specs=[rows_block(hps * V_DIM)]),
        out_shape=jax.ShapeDtypeStruct((b, s, N_HEADS * V_DIM), BF16),
        in_specs=[pl.BlockSpec(memory_space=pltpu.SMEM), hbm, hbm, hbm, vmem, vmem],
        out_specs=hbm,
        scratch_shapes=[pltpu.VMEM((N_HEADS, 2, t, t), F32),
                        pltpu.VMEM((hps, VAUG_ROWS, s), BF16)],
        compiler_params=pltpu.CompilerParams(vmem_limit_bytes=VMEM_LIMIT_BYTES),
        name="diff_attention",
    )(rel_bias, qt, k, vt, lam_params, subln_col)


def kernel(x, norm_mix, norm_mlp, pool_w, pool_scale, kv_norm, w_kv, k_norm, rel_bias,
           w_q, q_norm, lam_q1, lam_k1, lam_q2, lam_k2, subln, w_o, w_mlp_in, w_mlp_out):
    b, s, d = x.shape
    n = b * s
    row = lambda a: a.reshape(1, -1).astype(F32)

    seg = np.arange(MXU_TILE) // HEAD_DIM
    avg = jnp.asarray((seg[:, None] == seg[None, :]).astype(np.float32) / HEAD_DIM, dtype=BF16)

    assert s % TOKEN_TILE == 0 and TOKEN_TILE >= MAX_WINDOW
    x2 = x.reshape(n, d)
    for l in range(N_A):
        x2 = _pool_mlp_layer(x2, row(norm_mix[l]), pool_w, row(pool_scale[l]), row(norm_mlp[l]),
                             w_mlp_in, w_mlp_out, l, s // TOKEN_TILE)

    k = vt = None
    for l in range(N_A, DEPTH):
        j = l - N_A
        lambda_init = 0.8 - 0.6 * math.exp(-0.3 * l)
        q_norm_col = (q_norm[j].astype(F32) * (HEAD_DIM ** -0.5 * LOG2E)).reshape(HEAD_DIM, 1)
        x3 = x2.reshape(b, s, d)
        cast = [(w_o, j), (w_mlp_in, l), (w_mlp_out, l)]
        if l == N_A:
            k_gain = row(jnp.tile(k_norm, QK_WIDTH // HEAD_DIM))
            qt, k, vt, wo_bf, win_bf, wout_bf = _projections(
                x3, row(norm_mix[l]), w_q, j, q_norm_col, cast,
                kv=(row(kv_norm), w_kv, avg, k_gain))
        else:
            qt, wo_bf, win_bf, wout_bf = _projections(x3, row(norm_mix[l]), w_q, j, q_norm_col, cast)
        lam_params = jnp.stack([lam_q1[j], lam_k1[j], lam_q2[j], lam_k2[j]]).astype(F32)
        o = _diff_attention(rel_bias.astype(F32), qt, k, vt, lam_params,
                            subln[j].astype(F32).reshape(V_DIM, 1), lambda_init)
        x2 = _oproj_mlp_layer(x2, o.reshape(n, N_HEADS * V_DIM), wo_bf, row(norm_mlp[l]),
                              win_bf, wout_bf)
    return x2.reshape(b, s, d)
```

```python
import functools
import math

import numpy as np
import jax
import jax.numpy as jnp
from jax import lax
from jax.experimental import pallas as pl
from jax.experimental.pallas import tpu as pltpu

D_MODEL = 1024
DEPTH = 4
N_A = DEPTH // 2
POOL_WINDOWS = (2, 4, 8, 16)
N_GROUPS = len(POOL_WINDOWS)
GROUP_DIM = D_MODEL // N_GROUPS
N_HEADS = 8
HEAD_DIM = D_MODEL // (2 * N_HEADS)
V_DIM = 2 * HEAD_DIM
QK_WIDTH = N_HEADS * 2 * HEAD_DIM
D_FF = 4 * D_MODEL
N_BUCKETS = 32
MAX_DISTANCE = 128
EPS = 1e-6

LOG2E = math.log2(math.e)
MASK_VALUE = -1e30
MAX_WINDOW = max(POOL_WINDOWS)
LANES = 128
MXU_TILE = 256

VMEM_LIMIT_BYTES = 56 * 1024 * 1024

TOKEN_TILE = 512
FF_CHUNK = 1024
PROJ_TOKEN_TILE = 1024
OPROJ_TOKEN_TILE = 1024
ATTN_TILE = 256
HEADS_PER_STEP = 2
VAUG_ROWS = V_DIM + 16

F32 = jnp.float32
BF16 = jnp.bfloat16


def _bucket_thresholds():
    max_exact = N_BUCKETS // 2
    n = np.arange(1, 4 * MAX_DISTANCE, dtype=np.float64)
    v = np.log(n / max_exact) / math.log(MAX_DISTANCE / max_exact) * (N_BUCKETS - max_exact)
    frac = v - np.floor(v)
    margin = np.minimum(frac, 1.0 - frac)[max_exact:MAX_DISTANCE - 1]
    assert margin.min() > 1e-3
    large = np.minimum(max_exact + v.astype(np.int64), N_BUCKETS - 1)
    bucket = np.where(n < max_exact, n.astype(np.int64), large)
    thr = [0] + [int(np.argmax(bucket >= b)) + 1 for b in range(1, N_BUCKETS)]
    return tuple(thr)


BUCKET_START = _bucket_thresholds()
FAR_DISTANCE = BUCKET_START[-1]


def _rmsnorm(xf, g):
    ms = jnp.mean(xf * xf, axis=-1, keepdims=True)
    return xf * lax.rsqrt(ms + EPS) * g


def _dot(a, b):
    return jnp.dot(a, b, preferred_element_type=F32)


def _pool_prepare(x_ref, seq_tile_idx, g_ref, carry_ref, ext_ref):
    t = x_ref.shape[0]
    h = _rmsnorm(x_ref[...], g_ref[...])
    ext_ref[:MAX_WINDOW, :] = jnp.where(seq_tile_idx == 0, 0.0, carry_ref[...])
    ext_ref[MAX_WINDOW:, :] = h
    carry_ref[...] = h[t - MAX_WINDOW:, :]


def _pool_group(g, x_ref, seq_tile_idx, pw_ref, ps_ref, ext_ref, out_ref):
    t = x_ref.shape[0]
    w = POOL_WINDOWS[g]
    lo, hi = g * GROUP_DIM, (g + 1) * GROUP_DIM
    s = ext_ref[:, lo:hi]
    h = s[MAX_WINDOW:, :]
    span = 1
    while span < w:
        s = s + pltpu.roll(s, span, axis=0)
        span *= 2
    pos = seq_tile_idx * t + lax.broadcasted_iota(jnp.int32, (t, 1), 0)
    cnt = jnp.minimum(pos + 1, w).astype(F32)
    m = s[MAX_WINDOW:, :] / cnt - h
    y = _dot(m.astype(BF16), pw_ref[g].astype(BF16))
    out_ref[:, lo:hi] = x_ref[:, lo:hi] + y * ps_ref[:, lo:hi]


def _pool_mlp_kernel(x0_ref, xn_ref, gmix_ref, pw_ref, ps_ref, g_ref, win_ref, wout_ref, out_ref,
                     carry_ref, ext_ref, mixed_ref, *, tiles_per_seq):
    k = pl.program_id(0)
    last = pl.num_programs(0) - 1

    @pl.when(k == 0)
    def _():
        _pool_prepare(x0_ref, 0, gmix_ref, carry_ref, ext_ref)
        for g in range(N_GROUPS):
            _pool_group(g, x0_ref, 0, pw_ref, ps_ref, ext_ref, mixed_ref.at[0])

    nxt = jnp.minimum(k + 1, last) % tiles_per_seq
    _pool_prepare(xn_ref, nxt, gmix_ref, carry_ref, ext_ref)
    side_jobs = [functools.partial(_pool_group, g, xn_ref, nxt, pw_ref, ps_ref, ext_ref,
                                   mixed_ref.at[(k + 1) % 2]) for g in range(N_GROUPS)]
    out_ref[...] = _mlp(mixed_ref[k % 2], g_ref, win_ref, wout_ref, side_jobs)


def _pool_mlp_layer(x2, g_mix, pw, ps, g, w_in, w_out, layer, tiles_per_seq):
    n, d = x2.shape
    tm = TOKEN_TILE
    last = n // tm - 1
    return pl.pallas_call(
        functools.partial(_pool_mlp_kernel, tiles_per_seq=tiles_per_seq),
        out_shape=jax.ShapeDtypeStruct(x2.shape, x2.dtype),
        grid=(n // tm,),
        in_specs=[
            pl.BlockSpec((tm, d), lambda i: (0, 0), pipeline_mode=pl.Buffered(1)),
            pl.BlockSpec((tm, d), lambda i: (jnp.minimum(i + 1, last), 0)),
            _resident(g_mix.shape), _resident_layer(pw, layer), _resident(ps.shape),
            _resident(g.shape), _resident_layer(w_in, layer), _resident_layer(w_out, layer),
        ],
        out_specs=pl.BlockSpec((tm, d), lambda i: (i, 0)),
        scratch_shapes=[pltpu.VMEM((MAX_WINDOW, d), F32),
                        pltpu.VMEM((MAX_WINDOW + tm, d), F32),
                        pltpu.VMEM((2, tm, d), F32)],
        compiler_params=pltpu.CompilerParams(
            dimension_semantics=("arbitrary",), vmem_limit_bytes=VMEM_LIMIT_BYTES),
        name="pool_mlp",
    )(x2, x2, g_mix, pw, ps, g, w_in, w_out)


def _mlp(x, g_ref, win_ref, wout_ref, side_jobs=()):
    h = _rmsnorm(x, g_ref[...]).astype(BF16)
    acc = x
    for c in range(D_FF // FF_CHUNK):
        lo, hi = c * FF_CHUNK, (c + 1) * FF_CHUNK
        a = _dot(h, win_ref[:, lo:hi].astype(BF16))
        if c < len(side_jobs):
            side_jobs[c]()
        a = jnp.square(jnp.maximum(a, 0.0)).astype(BF16)
        acc = acc + _dot(a, wout_ref[lo:hi, :].astype(BF16))
    return acc


def _oproj_mlp_kernel(x_hbm, o_hbm, wo_ref, g_ref, win_ref, wout_ref, out_hbm, *, grid, row):
    def step(x_ref, o_ref, out_ref):
        x = x_ref[...] + _dot(o_ref[...], wo_ref[...].astype(BF16))
        out_ref[...] = _mlp(x, g_ref, win_ref, wout_ref)

    pltpu.emit_pipeline(step, grid=grid, in_specs=[row, row], out_specs=[row])(
        x_hbm, o_hbm, out_hbm)


def _resident(shape):
    zeros = (0,) * len(shape)
    return pl.BlockSpec(shape, lambda *_: zeros, pipeline_mode=pl.Buffered(1))


def _resident_layer(stacked, layer):
    index = (layer,) + (0,) * (stacked.ndim - 1)
    return pl.BlockSpec((None,) + stacked.shape[1:], lambda *_: index,
                        pipeline_mode=pl.Buffered(1))


def _oproj_mlp_layer(x2, o2, w_o, g, w_in, w_out):
    n, d = x2.shape
    tm = OPROJ_TOKEN_TILE
    row = pl.BlockSpec((tm, d), lambda i: (i, 0))
    hbm = pl.BlockSpec(memory_space=pl.ANY)
    vmem = pl.BlockSpec(memory_space=pltpu.VMEM)
    return pl.pallas_call(
        functools.partial(_oproj_mlp_kernel, grid=(n // tm,), row=row),
        out_shape=jax.ShapeDtypeStruct(x2.shape, x2.dtype),
        in_specs=[hbm, hbm, vmem, vmem, vmem, vmem],
        out_specs=hbm,
        compiler_params=pltpu.CompilerParams(vmem_limit_bytes=VMEM_LIMIT_BYTES),
        name="oproj_mlp",
    )(x2, o2, w_o, g, w_in, w_out)


def _proj_kernel(*refs, with_kv, n_cast):
    cast_in, refs = refs[:n_cast], refs[n_cast:]
    if with_kv:
        (x_ref, gq_ref, wq_ref, qn_ref, gkv_ref, wkv_ref, avg_ref, kg_ref,
         qt_ref, k_ref, vt_ref, *cast_out) = refs
    else:
        x_ref, gq_ref, wq_ref, qn_ref, qt_ref, *cast_out = refs
    for src, dst in zip(cast_in, cast_out):
        dst[...] = src[...].astype(BF16)
    x = x_ref[0]
    xn = x * lax.rsqrt(jnp.mean(x * x, axis=-1, keepdims=True) + EPS)

    def transposed_proj(w, h):
        return _dot(h, w).T

    qt = transposed_proj(wq_ref[...].astype(BF16), (xn * gq_ref[...]).astype(BF16))
    t = qt.shape[-1]
    qt = qt.reshape(QK_WIDTH // HEAD_DIM, HEAD_DIM, t)
    qms = jnp.mean(qt * qt, axis=1, keepdims=True)
    qt = qt * lax.rsqrt(qms + EPS) * qn_ref[...][None]
    qt_ref[0] = qt.reshape(QK_WIDTH, t).astype(BF16)

    if with_kv:
        h = (xn * gkv_ref[...]).astype(BF16)
        k = _dot(h, wkv_ref[:, :QK_WIDTH].astype(BF16))
        sq = (k * k).astype(BF16)
        kms = jnp.concatenate([_dot(sq[:, c:c + MXU_TILE], avg_ref[...])
                               for c in range(0, QK_WIDTH, MXU_TILE)], axis=1)
        k_ref[0] = (k * lax.rsqrt(kms + EPS) * kg_ref[...]).astype(BF16)
        vt_ref[0] = transposed_proj(wkv_ref[:, QK_WIDTH:].astype(BF16), h).astype(BF16)


def _projections(x, g_q, w_q, layer, q_norm_col, cast, kv=None):
    b, s, d = x.shape
    tm = PROJ_TOKEN_TILE
    steps_per_seq = s // tm
    n_steps = b * steps_per_seq
    tile = lambda width: pl.BlockSpec((1, tm, width), lambda i, j: (i, j, 0))
    args, specs, cast_shapes, cast_specs = [], [], [], []
    for w, w_layer in cast:
        rows, cols = w.shape[1] // n_steps, w.shape[2]
        assert w.shape[1] % n_steps == 0 and rows % 16 == 0
        args.append(w)
        specs.append(pl.BlockSpec((None, rows, cols),
                                  lambda i, j, w_layer=w_layer: (w_layer, i * steps_per_seq + j, 0)))
        cast_shapes.append(jax.ShapeDtypeStruct(w.shape[1:], BF16))
        cast_specs.append(pl.BlockSpec((rows, cols), lambda i, j: (i * steps_per_seq + j, 0)))
    tile_t = lambda height: pl.BlockSpec((1, height, tm), lambda i, j: (i, 0, j))
    args += [x, g_q, w_q, q_norm_col]
    specs += [tile(d), _resident(g_q.shape), _resident_layer(w_q, layer),
              _resident(q_norm_col.shape)]
    out_shape = [jax.ShapeDtypeStruct((b, QK_WIDTH, s), BF16)]
    out_specs = [tile_t(QK_WIDTH)]
    if kv is not None:
        g_kv, w_kv, avg, k_gain = kv
        v_width = w_kv.shape[1] - QK_WIDTH
        args += [g_kv, w_kv, avg, k_gain]
        specs += [_resident(g_kv.shape), _resident(w_kv.shape), _resident(avg.shape),
                  _resident(k_gain.shape)]
        out_shape += [jax.ShapeDtypeStruct((b, s, QK_WIDTH), BF16),
                      jax.ShapeDtypeStruct((b, v_width, s), BF16)]
        out_specs += [tile(QK_WIDTH), tile_t(v_width)]
    return pl.pallas_call(
        functools.partial(_proj_kernel, with_kv=kv is not None, n_cast=len(cast)),
        out_shape=tuple(out_shape + cast_shapes),
        grid=(b, steps_per_seq),
        in_specs=specs,
        out_specs=tuple(out_specs + cast_specs),
        compiler_params=pltpu.CompilerParams(
            dimension_semantics=("arbitrary", "arbitrary"), vmem_limit_bytes=VMEM_LIMIT_BYTES),
        name="qkv_proj" if kv is not None else "q_proj",
    )(*args)


def _build_bias_tiles(rb_ref, bias_ref, t):
    key = lax.broadcasted_iota(jnp.int32, (t, t), 0)
    query = lax.broadcasted_iota(jnp.int32, (t, t), 1)
    for hh in range(N_HEADS):
        last = rb_ref[N_BUCKETS - 1, hh]
        for kind, offset in enumerate((0, t)):
            rel = query - key + offset
            tile = jnp.full((t, t), rb_ref[0, hh], F32)
            for bkt in range(1, N_BUCKETS):
                tile = jnp.where(rel >= BUCKET_START[bkt], rb_ref[bkt, hh], tile)
            tile = (tile - last) * LOG2E
            if offset == 0:
                tile = jnp.where(rel >= 0, tile, MASK_VALUE)
            bias_ref[hh, kind] = tile


def _attn_kernel(rb_ref, qt_hbm, k_hbm, vt_hbm, lam_ref, sub_ref, o_hbm, bias_ref, vaug_ref,
                 *, lambda_init, grid, in_specs, out_specs):
    _build_bias_tiles(rb_ref, bias_ref, ATTN_TILE)
    step = functools.partial(_attn_step, lam_ref=lam_ref, sub_ref=sub_ref, bias_ref=bias_ref,
                             vaug_ref=vaug_ref, lambda_init=lambda_init)
    pltpu.emit_pipeline(step, grid=grid, in_specs=in_specs, out_specs=out_specs)(
        qt_hbm, k_hbm, vt_hbm, o_hbm)


def _attn_step(qt_ref, k_ref, vt_ref, o_ref, *, lam_ref, sub_ref, bias_ref, vaug_ref, lambda_init):
    t = ATTN_TILE
    seq = k_ref.shape[1]
    nq = seq // t
    qk, vd = 2 * HEAD_DIM, V_DIM
    head0 = pl.program_id(1) * HEADS_PER_STEP

    pad_row = lax.broadcasted_iota(jnp.int32, (VAUG_ROWS - vd, seq), 0)
    for a in range(HEADS_PER_STEP):
        vaug_ref[a, :vd, :] = vt_ref[0, a * vd:(a + 1) * vd, :]
        vaug_ref[a, vd:, :] = jnp.where(pad_row == 0, 1.0, 0.0).astype(BF16)

    lam_p = lam_ref[...]
    lam = (jnp.exp(jnp.sum(lam_p[0:1] * lam_p[1:2], axis=-1, keepdims=True))
           - jnp.exp(jnp.sum(lam_p[2:3] * lam_p[3:4], axis=-1, keepdims=True))
           + lambda_init)
    feat = lax.broadcasted_iota(jnp.int32, (qk, t), 0)

    def scores(a, i):
        qt = qt_ref[0, a * qk:(a + 1) * qk, i * t:(i + 1) * t]
        zero = jnp.zeros_like(qt)
        q_stack = jnp.concatenate([jnp.where(feat < HEAD_DIM, qt, zero),
                                   jnp.where(feat >= HEAD_DIM, qt, zero)], axis=1)
        return _dot(k_ref[0, :(i + 1) * t, a * qk:(a + 1) * qk], q_stack)

    def probabilities(a, i, s):
        hh = head0 + a
        rows = [s[kb * t:(kb + 1) * t, :] for kb in range(i + 1)]
        both = lambda tile: jnp.concatenate([tile, tile], axis=1)
        rows[i] = rows[i] + both(bias_ref[hh, 0])
        if i >= 1:
            rows[i - 1] = rows[i - 1] + both(bias_ref[hh, 1])
        sc = jnp.concatenate(rows, axis=0) if i >= 1 else rows[0]
        m = jnp.max(sc, axis=0, keepdims=True)
        return jnp.exp2(sc - m).astype(BF16)

    def output(a, i, p):
        acc = _dot(vaug_ref[a, :, :(i + 1) * t], p)
        outs = [acc[:vd, c * t:(c + 1) * t] / acc[vd:vd + 1, c * t:(c + 1) * t] for c in range(2)]
        ot = outs[0] - lam * outs[1]
        ms = jnp.mean(ot * ot, axis=0, keepdims=True)
        ot = ot * lax.rsqrt(ms + EPS) * sub_ref[...] * (1.0 - lambda_init)
        o_ref[0, i * t:(i + 1) * t, a * vd:(a + 1) * vd] = ot.T.astype(BF16)

    order = [1] + list(range(nq - 1, 1, -1)) + [0]
    assert sorted(order) == list(range(nq))
    blocks = [(a, i) for i in order for a in range(HEADS_PER_STEP)]
    s_cur = p_cur = None
    for n in range(len(blocks) + 2):
        p_next = probabilities(*blocks[n - 1], s_cur) if 1 <= n <= len(blocks) else None
        s_next = scores(*blocks[n]) if n < len(blocks) else None
        if n >= 2:
            output(*blocks[n - 2], p_cur)
        s_cur, p_cur = s_next, p_next


def _diff_attention(rel_bias, qt, k, vt, lam_params, subln_col, lambda_init):
    b, s, _ = k.shape
    t = ATTN_TILE
    hps = HEADS_PER_STEP
    assert t > FAR_DISTANCE and s % t == 0 and N_HEADS % hps == 0
    rows_block = lambda width: pl.BlockSpec((1, s, width), lambda i, j: (i, 0, j))
    cols_block = lambda height: pl.BlockSpec((1, height, s), lambda i, j: (i, j, 0),
                                             pipeline_mode=pl.Buffered(3))
    hbm = pl.BlockSpec(memory_space=pl.ANY)
    vmem = pl.BlockSpec(memory_space=pltpu.VMEM)
    return pl.pallas_call(
        functools.partial(
            _attn_kernel, lambda_init=lambda_init, grid=(b, N_HEADS // hps),
            in_specs=[cols_block(hps * 2 * HEAD_DIM), rows_block(hps * 2 * HEAD_DIM),
                      cols_block(hps * V_DIM)],
            out_specs=[rows_block(hps * V_DIM)]),
        out_shape=jax.ShapeDtypeStruct((b, s, N_HEADS * V_DIM), BF16),
        in_specs=[pl.BlockSpec(memory_space=pltpu.SMEM), hbm, hbm, hbm, vmem, vmem],
        out_specs=hbm,
        scratch_shapes=[pltpu.VMEM((N_HEADS, 2, t, t), F32),
                        pltpu.VMEM((hps, VAUG_ROWS, s), BF16)],
        compiler_params=pltpu.CompilerParams(vmem_limit_bytes=VMEM_LIMIT_BYTES),
        name="diff_attention",
    )(rel_bias, qt, k, vt, lam_params, subln_col)


def kernel(x, norm_mix, norm_mlp, pool_w, pool_scale, kv_norm, w_kv, k_norm, rel_bias,
           w_q, q_norm, lam_q1, lam_k1, lam_q2, lam_k2, subln, w_o, w_mlp_in, w_mlp_out):
    b, s, d = x.shape
    n = b * s
    row = lambda a: a.reshape(1, -1).astype(F32)

    seg = np.arange(MXU_TILE) // HEAD_DIM
    avg = jnp.asarray((seg[:, None] == seg[None, :]).astype(np.float32) / HEAD_DIM, dtype=BF16)

    assert s % TOKEN_TILE == 0 and TOKEN_TILE >= MAX_WINDOW
    x2 = x.reshape(n, d)
    for l in range(N_A):
        x2 = _pool_mlp_layer(x2, row(norm_mix[l]), pool_w, row(pool_scale[l]), row(norm_mlp[l]),
                             w_mlp_in, w_mlp_out, l, s // TOKEN_TILE)

    k = vt = None
    for l in range(N_A, DEPTH):
        j = l - N_A
        lambda_init = 0.8 - 0.6 * math.exp(-0.3 * l)
        q_norm_col = (q_norm[j].astype(F32) * (HEAD_DIM ** -0.5 * LOG2E)).reshape(HEAD_DIM, 1)
        x3 = x2.reshape(b, s, d)
        cast = [(w_o, j), (w_mlp_in, l), (w_mlp_out, l)]
        if l == N_A:
            k_gain = row(jnp.tile(k_norm, QK_WIDTH // HEAD_DIM))
            qt, k, vt, wo_bf, win_bf, wout_bf = _projections(
                x3, row(norm_mix[l]), w_q, j, q_norm_col, cast,
                kv=(row(kv_norm), w_kv, avg, k_gain))
        else:
            qt, wo_bf, win_bf, wout_bf = _projections(x3, row(norm_mix[l]), w_q, j, q_norm_col, cast)
        lam_params = jnp.stack([lam_q1[j], lam_k1[j], lam_q2[j], lam_k2[j]]).astype(F32)
        o = _diff_attention(rel_bias.astype(F32), qt, k, vt, lam_params,
                            subln[j].astype(F32).reshape(V_DIM, 1), lambda_init)
        x2 = _oproj_mlp_layer(x2, o.reshape(n, N_HEADS * V_DIM), wo_bf, row(norm_mlp[l]),
                              win_bf, wout_bf)
    return x2.reshape(b, s, d)
```
